```python
import jax, jax.numpy as jnp
from jax import lax
import numpy as np

D_MODEL = 4096
BATCH = 4
SEQ = 2048
DEPTH = 1

EPS = 1e-6
MOBA_HEAD_DIM = 128
MOBA_HEADS = D_MODEL // 256
MOBA_WIDTH = MOBA_HEADS * MOBA_HEAD_DIM
MOBA_BLOCK = 256
MOBA_TOPK = 3
MOBA_QCHUNK = 16
ROPE_THETA = 10000.0
GLA_HEADS = D_MODEL // 512
GLA_K_WIDTH = D_MODEL // 2
GLA_V_WIDTH = D_MODEL
GLA_DK = GLA_K_WIDTH // GLA_HEADS
GLA_DV = GLA_V_WIDTH // GLA_HEADS
GLA_GATE_RANK = 16
GLA_GATE_TAU = 16.0
GLA_CHUNK = 64
D_FF = -(-8 * D_MODEL // (3 * 256)) * 256
IN_SPLITS = (MOBA_WIDTH, MOBA_WIDTH, MOBA_WIDTH,
             GLA_K_WIDTH, GLA_K_WIDTH, GLA_V_WIDTH,
             GLA_V_WIDTH, GLA_GATE_RANK,
             D_MODEL, D_MODEL)
IN_COLS = sum(IN_SPLITS)

kernel_name = "moba_gla_parallel_gated_hybrid"


def _rmsnorm(x, g):
    xf = x.astype(jnp.float32)
    y = xf * lax.rsqrt(jnp.mean(xf * xf, axis=-1, keepdims=True) + EPS)
    return (y * g.astype(jnp.float32)).astype(x.dtype)


def _split_cols(u, sizes):
    idx = [int(v) for v in np.cumsum(sizes)[:-1]]
    return jnp.split(u, idx, axis=-1)


def _rope(x, pos):
    d = x.shape[-1]
    half = d // 2
    inv_freq = ROPE_THETA ** (-jnp.arange(half, dtype=jnp.float32) / half)
    ang = pos.astype(jnp.float32)[:, None] * inv_freq[None, :]
    cos = jnp.cos(ang)[None, :, None, :]
    sin = jnp.sin(ang)[None, :, None, :]
    xf = x.astype(jnp.float32)
    x1, x2 = xf[..., :half], xf[..., half:]
    out = jnp.concatenate([x1 * cos - x2 * sin, x2 * cos + x1 * sin], axis=-1)
    return out.astype(x.dtype)


def _moba_attention(q, k, v):
    B, S, H, D = q.shape
    s_pad = -(-S // MOBA_BLOCK) * MOBA_BLOCK
    pad = ((0, 0), (0, s_pad - S), (0, 0), (0, 0))
    q, k, v = [jnp.pad(t, pad).transpose(0, 2, 1, 3) for t in (q, k, v)]
    nb = s_pad // MOBA_BLOCK
    scale = D ** -0.5
    kb = k.reshape(B, H, nb, MOBA_BLOCK, D)
    vb = v.reshape(B, H, nb, MOBA_BLOCK, D)

    kmean = jnp.mean(kb.astype(jnp.float32), axis=3)
    gate = jnp.einsum('bhsd,bhnd->bhsn', q.astype(jnp.float32), kmean)
    q_blk = jnp.arange(s_pad) // MOBA_BLOCK
    past = jnp.arange(nb)[None, :] < q_blk[:, None]
    gate = jnp.where(past[None, None], gate, -jnp.inf)
    k_sel = min(MOBA_TOPK, nb)
    top_val, top_idx = lax.top_k(gate, k_sel)
    top_ok = jnp.isfinite(top_val)

    n_chunks = s_pad // MOBA_QCHUNK
    qc = q.reshape(B, H, n_chunks, MOBA_QCHUNK, D).transpose(2, 0, 1, 3, 4)
    idx_c = top_idx.reshape(B, H, n_chunks, MOBA_QCHUNK, k_sel).transpose(2, 0, 1, 3, 4)
    ok_c = top_ok.reshape(B, H, n_chunks, MOBA_QCHUNK, k_sel).transpose(2, 0, 1, 3, 4)
    b_ix = jnp.arange(B)[:, None, None, None]
    h_ix = jnp.arange(H)[None, :, None, None]

    def chunk(args):
        c, qi, ii, oki = args
        q0 = c * MOBA_QCHUNK
        own = q0 // MOBA_BLOCK
        k_g = kb[b_ix, h_ix, ii]
        v_g = vb[b_ix, h_ix, ii]
        s_sel = jnp.einsum('bhqd,bhqtkd->bhqtk', qi, k_g).astype(jnp.float32) * scale
        s_sel = jnp.where(oki[..., None], s_sel, -jnp.inf)
        s_sel = s_sel.reshape(B, H, MOBA_QCHUNK, k_sel * MOBA_BLOCK)
        k_own = lax.dynamic_index_in_dim(kb, own, axis=2, keepdims=False)
        v_own = lax.dynamic_index_in_dim(vb, own, axis=2, keepdims=False)
        s_own = jnp.einsum('bhqd,bhkd->bhqk', qi, k_own).astype(jnp.float32) * scale
        qpos = q0 + jnp.arange(MOBA_QCHUNK)
        kpos = own * MOBA_BLOCK + jnp.arange(MOBA_BLOCK)
        s_own = jnp.where((kpos[None, :] <= qpos[:, None])[None, None], s_own, -jnp.inf)
        p = jax.nn.softmax(jnp.concatenate([s_sel, s_own], axis=-1), axis=-1).astype(v.dtype)
        p_sel = p[..., :k_sel * MOBA_BLOCK].reshape(B, H, MOBA_QCHUNK, k_sel, MOBA_BLOCK)
        p_own = p[..., k_sel * MOBA_BLOCK:]
        return (jnp.einsum('bhqtk,bhqtkd->bhqd', p_sel, v_g)
                + jnp.einsum('bhqk,bhkd->bhqd', p_own, v_own))

    o = lax.map(chunk, (jnp.arange(n_chunks), qc, idx_c, ok_c))
    o = o.transpose(1, 0, 3, 2, 4).reshape(B, s_pad, H * D)
    return o[:, :S]


def _gla(q, k, v, log_a):
    B, S, H, DK = q.shape
    DV = v.shape[-1]
    nc = S // GLA_CHUNK

    def chunked(t):
        return t.astype(jnp.float32).reshape(B, nc, GLA_CHUNK, H, t.shape[-1]).transpose(1, 0, 3, 2, 4)

    qc = chunked(q) * (DK ** -0.5)
    kc = chunked(k)
    vc = chunked(v)
    bcum = jnp.cumsum(chunked(log_a), axis=3)
    b_last = bcum[:, :, :, -1:, :]
    q_dec = qc * jnp.exp(bcum)
    k_inv = kc * jnp.exp(-bcum)
    k_to_end = kc * jnp.exp(b_last - bcum)
    causal = jnp.tril(jnp.ones((GLA_CHUNK, GLA_CHUNK), dtype=bool))
    attn = jnp.einsum('nbhid,nbhjd->nbhij', q_dec, k_inv)
    attn = jnp.where(causal, attn, 0.0)
    o_intra = jnp.einsum('nbhij,nbhjv->nbhiv', attn, vc)

    def step(state, inp):
        q_d, k_e, v_c, dl = inp
        o = jnp.einsum('bhid,bhdv->bhiv', q_d, state)
        state = jnp.exp(dl)[:, :, 0, :, None] * state + jnp.einsum('bhjd,bhjv->bhdv', k_e, v_c)
        return state, o

    s0 = jnp.zeros((B, H, DK, DV), jnp.float32)
    _, o_inter = lax.scan(step, s0, (q_dec, k_to_end, vc, b_last))
    o = o_intra + o_inter
    return o.transpose(1, 0, 3, 2, 4).reshape(B, S, H, DV)


def _mixer(h, w_in, gla_gate_up, gla_gate_bias, gla_out_norm_g,
           w_branch_moba, w_branch_gla, w_out):
    B, S, _ = h.shape
    u = h @ w_in
    mq, mk, mv, gq, gk, gv, gr, ga_down, g_moba, g_gla = _split_cols(u, IN_SPLITS)
    pos = jnp.arange(S)
    mq = _rope(mq.reshape(B, S, MOBA_HEADS, MOBA_HEAD_DIM), pos)
    mk = _rope(mk.reshape(B, S, MOBA_HEADS, MOBA_HEAD_DIM), pos)
    mv = mv.reshape(B, S, MOBA_HEADS, MOBA_HEAD_DIM)
    y_moba = _moba_attention(mq, mk, mv) @ w_branch_moba
    log_a = jax.nn.log_sigmoid((ga_down @ gla_gate_up + gla_gate_bias).astype(jnp.float32)) / GLA_GATE_TAU
    o = _gla(gq.reshape(B, S, GLA_HEADS, GLA_DK),
             gk.reshape(B, S, GLA_HEADS, GLA_DK),
             gv.reshape(B, S, GLA_HEADS, GLA_DV),
             log_a.reshape(B, S, GLA_HEADS, GLA_DK)).astype(h.dtype)
    o = _rmsnorm(o, gla_out_norm_g).reshape(B, S, GLA_V_WIDTH) * jax.nn.silu(gr)
    y_gla = o @ w_branch_gla
    merged = jax.nn.sigmoid(g_moba) * y_moba + jax.nn.sigmoid(g_gla) * y_gla
    return merged @ w_out


def _swiglu(h, w_gate, w_up, w_down):
    return (jax.nn.silu(h @ w_gate) * (h @ w_up)) @ w_down


def setup_inputs(seed: int = 0) -> dict:
    key = jax.random.key(seed)
    ks = jax.random.split(key, 16)
    f32 = jnp.float32

    def dense(k, fan_in, fan_out):
        return jax.random.normal(k, (DEPTH, fan_in, fan_out), f32) * fan_in ** -0.5

    def gain(k, n):
        return 1.0 + 0.02 * jax.random.normal(k, (DEPTH, n), f32)

    return {
        "x": jax.random.normal(ks[0], (BATCH, SEQ, D_MODEL), f32),
        "pre_mix_norm_g": gain(ks[1], D_MODEL),
        "w_in": dense(ks[2], D_MODEL, IN_COLS),
        "gla_gate_up": dense(ks[3], GLA_GATE_RANK, GLA_K_WIDTH),
        "gla_gate_bias": 0.1 * jax.random.normal(ks[4], (DEPTH, GLA_K_WIDTH), f32),
        "gla_out_norm_g": gain(ks[5], GLA_DV),
        "w_branch_moba": dense(ks[6], MOBA_WIDTH, D_MODEL),
        "w_branch_gla": dense(ks[7], GLA_V_WIDTH, D_MODEL),
        "w_out": dense(ks[8], D_MODEL, D_MODEL),
        "post_mix_norm_g": gain(ks[9], D_MODEL),
        "pre_ffn_norm_g": gain(ks[10], D_MODEL),
        "w_ffn_gate": dense(ks[11], D_MODEL, D_FF),
        "w_ffn_up": dense(ks[12], D_MODEL, D_FF),
        "w_ffn_down": dense(ks[13], D_FF, D_MODEL),
        "post_ffn_norm_g": gain(ks[14], D_MODEL),
    }


def reference(x, pre_mix_norm_g, w_in, gla_gate_up, gla_gate_bias, gla_out_norm_g,
              w_branch_moba, w_branch_gla, w_out, post_mix_norm_g, pre_ffn_norm_g,
              w_ffn_gate, w_ffn_up, w_ffn_down, post_ffn_norm_g):
    for layer in range(DEPTH):
        h = _rmsnorm(x, pre_mix_norm_g[layer])
        y = _mixer(h, w_in[layer], gla_gate_up[layer], gla_gate_bias[layer],
                   gla_out_norm_g[layer], w_branch_moba[layer], w_branch_gla[layer],
                   w_out[layer])
        x = x + _rmsnorm(y, post_mix_norm_g[layer])
        h = _rmsnorm(x, pre_ffn_norm_g[layer])
        y = _swiglu(h, w_ffn_gate[layer], w_ffn_up[layer], w_ffn_down[layer])
        x = x + _rmsnorm(y, post_ffn_norm_g[layer])
    return x
```

```python
import functools

import jax
import jax.numpy as jnp
from jax import lax
from jax.experimental import pallas as pl
from jax.experimental.pallas import tpu as pltpu

F32 = jnp.float32
BF16 = jnp.bfloat16

EPS = 1e-6
MOBA_HEAD_DIM = 128
MOBA_BLOCK = 256
MOBA_TOPK = 3
ROPE_THETA = 10000.0
GLA_GATE_RANK = 16
GLA_GATE_TAU = 16.0
GLA_CHUNK = 64
LANES = 128
VMEM_LIMIT = 56 * 1024 * 1024

_NT = (((1,), (1,)), ((), ()))
_TN = (((0,), (0,)), ((), ()))


def _params(*sem):
    return pltpu.CompilerParams(dimension_semantics=sem, vmem_limit_bytes=VMEM_LIMIT)


def _rms(x, g):
    ms = jnp.mean(x * x, axis=-1, keepdims=True)
    return x * lax.rsqrt(ms + EPS) * g


def _rmsnorm_body(x_ref, g_ref, o_ref):
    o_ref[...] = _rms(x_ref[...], g_ref[...]).astype(o_ref.dtype)


def _rmsnorm(x, g, tm):
    m, d = x.shape
    return pl.pallas_call(
        _rmsnorm_body,
        grid=(m // tm,),
        in_specs=[pl.BlockSpec((tm, d), lambda i: (i, 0)),
                  pl.BlockSpec((1, d), lambda i: (0, 0))],
        out_specs=pl.BlockSpec((tm, d), lambda i: (i, 0)),
        out_shape=jax.ShapeDtypeStruct((m, d), BF16),
        compiler_params=_params("parallel"),
        name="rmsnorm",
    )(x, g)


def _mid_norm_body(y_ref, x_ref, g1_ref, g2_ref, x1_ref, h_ref):
    x1 = x_ref[...] + _rms(y_ref[...], g1_ref[...])
    x1_ref[...] = x1
    h_ref[...] = _rms(x1, g2_ref[...]).astype(h_ref.dtype)


def _mid_norm(y, x, g1, g2, tm):
    m, d = x.shape
    row = pl.BlockSpec((tm, d), lambda i: (i, 0))
    vec = pl.BlockSpec((1, d), lambda i: (0, 0))
    return pl.pallas_call(
        _mid_norm_body,
        grid=(m // tm,),
        in_specs=[row, row, vec, vec],
        out_specs=[row, row],
        out_shape=[jax.ShapeDtypeStruct((m, d), F32), jax.ShapeDtypeStruct((m, d), BF16)],
        compiler_params=_params("parallel"),
        name="mid_norm",
    )(y, x, g1, g2)


def _final_norm_body(y_ref, x_ref, g_ref, o_ref):
    o_ref[...] = x_ref[...] + _rms(y_ref[...], g_ref[...])


def _final_norm(y, x, g, tm):
    m, d = x.shape
    row = pl.BlockSpec((tm, d), lambda i: (i, 0))
    return pl.pallas_call(
        _final_norm_body,
        grid=(m // tm,),
        in_specs=[row, row, pl.BlockSpec((1, d), lambda i: (0, 0))],
        out_specs=row,
        out_shape=jax.ShapeDtypeStruct((m, d), F32),
        compiler_params=_params("parallel"),
        name="final_norm",
    )(y, x, g)


def _mm_body(x_ref, w_ref, o_ref):
    o_ref[...] = jnp.dot(x_ref[...], w_ref[...], preferred_element_type=F32).astype(o_ref.dtype)


def _matmul(x, w, n, col_block_off, tm, tn, out_dtype, name):
    m, k = x.shape
    return pl.pallas_call(
        _mm_body,
        grid=(n // tn, m // tm),
        in_specs=[pl.BlockSpec((tm, k), lambda j, i: (i, 0)),
                  pl.BlockSpec((k, tn), lambda j, i: (0, j + col_block_off))],
        out_specs=pl.BlockSpec((tm, tn), lambda j, i: (i, j)),
        out_shape=jax.ShapeDtypeStruct((m, n), out_dtype),
        compiler_params=_params("parallel", "parallel"),
        name=name,
    )(x, w)


def _merge_body(om_ref, og_ref, wm_ref, wg_ref, gm_ref, gg_ref, o_ref):
    ym = jnp.dot(om_ref[...], wm_ref[...], preferred_element_type=F32)
    yg = jnp.dot(og_ref[...], wg_ref[...], preferred_element_type=F32)
    merged = (jax.nn.sigmoid(gm_ref[...].astype(F32)) * ym
              + jax.nn.sigmoid(gg_ref[...].astype(F32)) * yg)
    o_ref[...] = merged.astype(o_ref.dtype)


def _merge(o_moba, o_gla, w_moba, w_gla, gates, tm, tn):
    m, km = o_moba.shape
    kg = o_gla.shape[1]
    d = w_moba.shape[1]
    nj = d // tn
    return pl.pallas_call(
        _merge_body,
        grid=(nj, m // tm),
        in_specs=[pl.BlockSpec((tm, km), lambda j, i: (i, 0)),
                  pl.BlockSpec((tm, kg), lambda j, i: (i, 0)),
                  pl.BlockSpec((km, tn), lambda j, i: (0, j)),
                  pl.BlockSpec((kg, tn), lambda j, i: (0, j)),
                  pl.BlockSpec((tm, tn), lambda j, i: (i, j)),
                  pl.BlockSpec((tm, tn), lambda j, i: (i, j + nj))],
        out_specs=pl.BlockSpec((tm, tn), lambda j, i: (i, j)),
        out_shape=jax.ShapeDtypeStruct((m, d), BF16),
        compiler_params=_params("parallel", "parallel"),
        name="branch_merge",
    )(o_moba, o_gla, w_moba, w_gla, gates, gates)


def _ffn_up_body(h_ref, wg_ref, wu_ref, o_ref):
    h = h_ref[...]
    a = jnp.dot(h, wg_ref[...], preferred_element_type=F32)
    b = jnp.dot(h, wu_ref[...], preferred_element_type=F32)
    o_ref[...] = (a * jax.nn.sigmoid(a) * b).astype(o_ref.dtype)


def _ffn_up(h, w_gate, w_up, tm, tn):
    m, k = h.shape
    n = w_gate.shape[1]
    wspec = pl.BlockSpec((k, tn), lambda j, i: (0, j))
    return pl.pallas_call(
        _ffn_up_body,
        grid=(n // tn, m // tm),
        in_specs=[pl.BlockSpec((tm, k), lambda j, i: (i, 0)), wspec, wspec],
        out_specs=pl.BlockSpec((tm, tn), lambda j, i: (i, j)),
        out_shape=jax.ShapeDtypeStruct((m, n), BF16),
        compiler_params=_params("parallel", "parallel"),
        name="ffn_up",
    )(h, w_gate, w_up)


def _mm_acc_body(x_ref, w_ref, o_ref, acc_ref):
    kk = pl.program_id(2)

    @pl.when(kk == 0)
    def _():
        acc_ref[...] = jnp.zeros_like(acc_ref)

    acc_ref[...] += jnp.dot(x_ref[...], w_ref[...], preferred_element_type=F32)

    @pl.when(kk == pl.num_programs(2) - 1)
    def _():
        o_ref[...] = acc_ref[...].astype(o_ref.dtype)


def _matmul_ksplit(x, w, tm, tn, tk, out_dtype, name):
    m, k = x.shape
    n = w.shape[1]
    return pl.pallas_call(
        _mm_acc_body,
        grid=(n // tn, m // tm, k // tk),
        in_specs=[pl.BlockSpec((tm, tk), lambda j, i, kk: (i, kk)),
                  pl.BlockSpec((tk, tn), lambda j, i, kk: (kk, j))],
        out_specs=pl.BlockSpec((tm, tn), lambda j, i, kk: (i, j)),
        out_shape=jax.ShapeDtypeStruct((m, n), out_dtype),
        scratch_shapes=[pltpu.VMEM((tm, tn), F32)],
        compiler_params=_params("parallel", "parallel", "arbitrary"),
        name=name,
    )(x, w)


def _moba_body(q_ref, k_ref, v_ref, cos_ref, sin_ref, o_ref, *, nb):
    blk = MOBA_BLOCK
    hd = MOBA_HEAD_DIM
    scale = hd ** -0.5
    neg_inf = -jnp.inf
    cos = cos_ref[...]
    sin = sin_ref[...]

    def rope(t):
        t = t.astype(F32)
        return t * cos + pltpu.roll(t, hd // 2, axis=1) * sin

    qr = rope(q_ref[0])
    kr = rope(k_ref[0])
    q_bf = qr.astype(BF16)
    k_bf = kr.astype(BF16)
    v_bf = v_ref[0]

    kmean = jnp.mean(kr.reshape(nb, blk, hd), axis=1)
    kmean_rep = jnp.concatenate([kmean] * (LANES // nb), axis=0)
    gate = lax.dot_general(qr, kmean_rep, _NT, precision=lax.Precision.HIGHEST,
                           preferred_element_type=F32)

    lane = lax.broadcasted_iota(jnp.int32, (blk, LANES), 1)
    n_of = lane % nb
    row = lax.broadcasted_iota(jnp.int32, (blk, blk), 0)
    col = lax.broadcasted_iota(jnp.int32, (blk, blk), 1)

    for i in range(nb):
        rows = slice(i * blk, (i + 1) * blk)
        g = jnp.where(n_of < i, gate[rows], neg_inf)
        rank = jnp.zeros((blk, LANES), jnp.int32)
        for j in range(1, nb):
            other = pltpu.roll(g, j, axis=1)
            m_of = (n_of - j + nb) % nb
            beats = (other > g) | ((other == g) & (m_of < n_of))
            rank = rank + beats.astype(jnp.int32)
        keep = (rank < MOBA_TOPK) & (jnp.abs(g) < jnp.inf)
        bias = jnp.where(keep, 0.0, neg_inf)

        kv = slice(0, (i + 1) * blk)
        s = lax.dot_general(q_bf[rows], k_bf[kv], _NT, preferred_element_type=F32) * scale
        pieces = [s[:, n * blk:(n + 1) * blk] + bias[:, n:n + 1] for n in range(i)]
        pieces.append(jnp.where(col <= row, s[:, i * blk:], neg_inf))
        s = jnp.concatenate(pieces, axis=1) if i else pieces[0]
        m = jnp.max(s, axis=1, keepdims=True)
        p = jnp.exp(s - m)
        l = jnp.sum(p, axis=1, keepdims=True)
        o = jnp.dot(p.astype(BF16), v_bf[kv], preferred_element_type=F32) / l
        o_ref[0, rows, :] = o.astype(o_ref.dtype)


def _moba(u, cos, sin, batch, seq, heads):
    hd = MOBA_HEAD_DIM
    nb = seq // MOBA_BLOCK
    blk3 = lambda off: pl.BlockSpec((1, seq, hd), lambda b, h: (b, 0, h + off))
    tab = pl.BlockSpec((seq, hd), lambda b, h: (0, 0))
    return pl.pallas_call(
        functools.partial(_moba_body, nb=nb),
        grid=(batch, heads),
        in_specs=[blk3(0), blk3(heads), blk3(2 * heads), tab, tab],
        out_specs=pl.BlockSpec((1, seq, hd), lambda b, h: (b, 0, h)),
        out_shape=jax.ShapeDtypeStruct((batch, seq, heads * hd), BF16),
        compiler_params=_params("parallel", "parallel"),
        name="moba_attention",
    )(u, u, u, cos, sin)


def _gla_body(q_ref, k_ref, v_ref, gr_ref, ga_ref, up_ref, bias_ref, g_ref, o_ref, st_ref, *, dk):
    c = GLA_CHUNK
    scale = dk ** -0.5
    hi = lax.Precision.HIGHEST
    st_ref[...] = jnp.zeros_like(st_ref)
    row = lax.broadcasted_iota(jnp.int32, (c, c), 0)
    col = lax.broadcasted_iota(jnp.int32, (c, c), 1)
    causal = col <= row
    tri = causal.astype(F32)

    def chunk_step(ci, carry):
        rows = pl.ds(pl.multiple_of(ci * c, c), c)
        z = jnp.dot(ga_ref[0, rows, :], up_ref[...], precision=hi,
                    preferred_element_type=F32) + bias_ref[...]
        log_a = (jnp.minimum(z, 0.0) - jnp.log1p(jnp.exp(-jnp.abs(z)))) / GLA_GATE_TAU
        b = jnp.dot(tri, log_a, precision=hi, preferred_element_type=F32)
        b_last = b[c - 1:c, :]
        q = q_ref[0, rows, :].astype(F32) * scale
        k = k_ref[0, rows, :].astype(F32)
        v = v_ref[0, rows, :]
        q_dec = (q * jnp.exp(b)).astype(BF16)
        k_inv = (k * jnp.exp(-b)).astype(BF16)
        k_end = (k * jnp.exp(b_last - b)).astype(BF16)
        attn = lax.dot_general(q_dec, k_inv, _NT, preferred_element_type=F32)
        attn = jnp.where(causal, attn, 0.0).astype(BF16)
        st = st_ref[...]
        o = (jnp.dot(attn, v, preferred_element_type=F32)
             + lax.dot_general(q_dec, st.astype(BF16), _NT, preferred_element_type=F32))
        st_ref[...] = st * jnp.exp(b_last) + lax.dot_general(v, k_end, _TN,
                                                             preferred_element_type=F32)
        gr = gr_ref[0, rows, :].astype(F32)
        y = _rms(o, g_ref[...]) * (gr * jax.nn.sigmoid(gr))
        o_ref[0, rows, :] = y.astype(o_ref.dtype)
        return carry

    lax.fori_loop(0, q_ref.shape[1] // c, chunk_step, 0)


def _gla(u, ga, up, bias, g, batch, seq, heads, dk, dv, q_off, k_off, v_off, r_off):
    qk = lambda off: pl.BlockSpec((1, seq, dk), lambda b, h: (b, 0, h + off))
    vv = lambda off: pl.BlockSpec((1, seq, dv), lambda b, h: (b, 0, h + off))
    return pl.pallas_call(
        functools.partial(_gla_body, dk=dk),
        grid=(batch, heads),
        in_specs=[qk(q_off), qk(k_off), vv(v_off), vv(r_off),
                  pl.BlockSpec((1, seq, LANES), lambda b, h: (b, 0, 0)),
                  pl.BlockSpec((LANES, dk), lambda b, h: (0, h)),
                  pl.BlockSpec((1, dk), lambda b, h: (0, h)),
                  pl.BlockSpec((1, dv), lambda b, h: (0, 0))],
        out_specs=pl.BlockSpec((1, seq, dv), lambda b, h: (b, 0, h)),
        out_shape=jax.ShapeDtypeStruct((batch, seq, heads * dv), BF16),
        scratch_shapes=[pltpu.VMEM((dv, dk), F32)],
        compiler_params=_params("parallel", "parallel"),
        name="gla",
    )(u, u, u, u, ga, up, bias, g)


def _tile(n, pref):
    return pref if n % pref == 0 else n


def _layer(x, pre_mix_g, w_in, gate_up, gate_bias, gla_norm_g, w_moba, w_gla, w_out,
           post_mix_g, pre_ffn_g, w_fg, w_fu, w_fd, post_ffn_g):
    batch, seq, d = x.shape
    t = batch * seq
    half = d // 2
    moba_heads = half // MOBA_HEAD_DIM
    gla_heads = d // 512
    dk = half // gla_heads
    dv = d // gla_heads
    n_main = 9 * half
    n_gate0 = n_main + GLA_GATE_RANK
    d_ff = w_fg.shape[1]

    tm = _tile(t, 1024)
    tn = _tile(d, 1024)
    tr = _tile(t, 256)
    x2 = x.reshape(t, d)
    vec = lambda g: g.reshape(1, -1)

    w_in_bf = w_in.astype(BF16)
    w_gates = w_in_bf[:, n_gate0:]
    w_ga = jnp.pad(w_in_bf[:, n_main:n_gate0], ((0, 0), (0, LANES - GLA_GATE_RANK)))
    up_pad = jnp.pad(gate_up, ((0, LANES - GLA_GATE_RANK), (0, 0)))
    ff_pad = -(-d_ff // 512) * 512 - d_ff
    w_fg_bf = jnp.pad(w_fg.astype(BF16), ((0, 0), (0, ff_pad)))
    w_fu_bf = jnp.pad(w_fu.astype(BF16), ((0, 0), (0, ff_pad)))
    w_fd_bf = jnp.pad(w_fd.astype(BF16), ((0, ff_pad), (0, 0)))

    hd2 = MOBA_HEAD_DIM // 2
    inv_freq = ROPE_THETA ** (-jnp.arange(hd2, dtype=F32) / hd2)
    ang = jnp.arange(seq, dtype=F32)[:, None] * inv_freq[None, :]
    cos = jnp.concatenate([jnp.cos(ang), jnp.cos(ang)], axis=1)
    sin = jnp.concatenate([-jnp.sin(ang), jnp.sin(ang)], axis=1)

    h = _rmsnorm(x2, vec(pre_mix_g), tr)
    u = _matmul(h, w_in_bf, n_main, 0, tm, 1024 if n_main % 1024 == 0 else 512, BF16,
                "in_proj_main")
    gates = _matmul(h, w_gates, 2 * d, 0, tm, tn, BF16, "in_proj_gates")
    ga = _matmul(h, w_ga, LANES, 0, tm, LANES, F32, "in_proj_forget")

    u3 = u.reshape(batch, seq, n_main)
    o_moba = _moba(u3, cos, sin, batch, seq, moba_heads)
    o_gla = _gla(u3, ga.reshape(batch, seq, LANES), up_pad, vec(gate_bias), vec(gla_norm_g),
                 batch, seq, gla_heads, dk, dv,
                 q_off=3 * half // dk, k_off=4 * half // dk, v_off=5 * half // dv,
                 r_off=7 * half // dv)

    merged = _merge(o_moba.reshape(t, half), o_gla.reshape(t, d),
                    w_moba.astype(BF16), w_gla.astype(BF16), gates, _tile(t, 512), tn)
    y = _matmul(merged, w_out.astype(BF16), d, 0, tm, tn, F32, "out_proj")
    x1, h2 = _mid_norm(y, x2, vec(post_mix_g), vec(pre_ffn_g), tr)

    a = _ffn_up(h2, w_fg_bf, w_fu_bf, tm, 512)
    kf = d_ff + ff_pad
    tk = kf // 4 if (kf // 4) % LANES == 0 else kf
    y2 = _matmul_ksplit(a, w_fd_bf, tm, tn, tk, F32, "ffn_down")
    out = _final_norm(y2, x1, vec(post_ffn_g), tr)
    return out.reshape(batch, seq, d)


def kernel(x, pre_mix_norm_g, w_in, gla_gate_up, gla_gate_bias, gla_out_norm_g, w_branch_moba,
           w_branch_gla, w_out, post_mix_norm_g, pre_ffn_norm_g, w_ffn_gate, w_ffn_up, w_ffn_down,
           post_ffn_norm_g):
    for layer in range(w_in.shape[0]):
        x = _layer(x, pre_mix_norm_g[layer], w_in[layer], gla_gate_up[layer], gla_gate_bias[layer],
                   gla_out_norm_g[layer], w_branch_moba[layer], w_branch_gla[layer], w_out[layer],
                   post_mix_norm_g[layer], pre_ffn_norm_g[layer], w_ffn_gate[layer],
                   w_ffn_up[layer], w_ffn_down[layer], post_ffn_norm_g[layer])
    return x
```

```python
import functools

import jax
import jax.numpy as jnp
from jax import lax
from jax.experimental import pallas as pl
from jax.experimental.pallas import tpu as pltpu

F32 = jnp.float32
BF16 = jnp.bfloat16

EPS = 1e-6
MOBA_HEAD_DIM = 128
MOBA_BLOCK = 256
MOBA_TOPK = 3
ROPE_THETA = 10000.0
GLA_GATE_RANK = 16
GLA_GATE_TAU = 16.0
GLA_CHUNK = 64
GLA_BLOCK_CHUNKS = 4
LANES = 128
VMEM_LIMIT = 60 * 1024 * 1024

_NT = (((1,), (1,)), ((), ()))
_TN = (((0,), (0,)), ((), ()))


def _params(*sem):
    return pltpu.CompilerParams(dimension_semantics=sem, vmem_limit_bytes=VMEM_LIMIT)


def _rms(x, g):
    ms = jnp.mean(x * x, axis=-1, keepdims=True)
    return x * lax.rsqrt(ms + EPS) * g


def _rmsnorm_body(x_ref, g_ref, o_ref):
    o_ref[...] = _rms(x_ref[...], g_ref[...]).astype(o_ref.dtype)


def _rmsnorm(x, g, tm):
    m, d = x.shape
    return pl.pallas_call(
        _rmsnorm_body,
        grid=(m // tm,),
        in_specs=[pl.BlockSpec((tm, d), lambda i: (i, 0)),
                  pl.BlockSpec((1, d), lambda i: (0, 0))],
        out_specs=pl.BlockSpec((tm, d), lambda i: (i, 0)),
        out_shape=jax.ShapeDtypeStruct((m, d), BF16),
        compiler_params=_params("parallel"),
        name="rmsnorm",
    )(x, g)


def _mid_norm_body(y_ref, x_ref, g1_ref, g2_ref, x1_ref, h_ref):
    x1 = x_ref[...] + _rms(y_ref[...].astype(F32), g1_ref[...])
    x1_ref[...] = x1
    h_ref[...] = _rms(x1, g2_ref[...]).astype(h_ref.dtype)


def _mid_norm(y, x, g1, g2, tm):
    m, d = x.shape
    row = pl.BlockSpec((tm, d), lambda i: (i, 0))
    vec = pl.BlockSpec((1, d), lambda i: (0, 0))
    return pl.pallas_call(
        _mid_norm_body,
        grid=(m // tm,),
        in_specs=[row, row, vec, vec],
        out_specs=[row, row],
        out_shape=[jax.ShapeDtypeStruct((m, d), F32), jax.ShapeDtypeStruct((m, d), BF16)],
        compiler_params=_params("parallel"),
        name="mid_norm",
    )(y, x, g1, g2)


def _final_norm_body(y_ref, x_ref, g_ref, o_ref):
    o_ref[...] = x_ref[...] + _rms(y_ref[...].astype(F32), g_ref[...])


def _final_norm(y, x, g, tm):
    m, d = x.shape
    row = pl.BlockSpec((tm, d), lambda i: (i, 0))
    return pl.pallas_call(
        _final_norm_body,
        grid=(m // tm,),
        in_specs=[row, row, pl.BlockSpec((1, d), lambda i: (0, 0))],
        out_specs=row,
        out_shape=jax.ShapeDtypeStruct((m, d), F32),
        compiler_params=_params("parallel"),
        name="final_norm",
    )(y, x, g)


def _cast_resident(w_ref, wbf_ref):
    @pl.when(pl.program_id(1) == 0)
    def _():
        wbf_ref[...] = w_ref[...].astype(BF16)


def _mm_body(x_ref, w_ref, o_ref, wbf_ref):
    _cast_resident(w_ref, wbf_ref)
    o_ref[...] = jnp.dot(x_ref[...], wbf_ref[...], preferred_element_type=F32).astype(o_ref.dtype)


def _matmul(x, w, n, tm, tn, out_dtype, name):
    m, k = x.shape
    return pl.pallas_call(
        _mm_body,
        grid=(n // tn, m // tm),
        in_specs=[pl.BlockSpec((tm, k), lambda j, i: (i, 0)),
                  pl.BlockSpec((k, tn), lambda j, i: (0, j))],
        out_specs=pl.BlockSpec((tm, tn), lambda j, i: (i, j)),
        out_shape=jax.ShapeDtypeStruct((m, n), out_dtype),
        scratch_shapes=[pltpu.VMEM((k, tn), BF16)],
        compiler_params=_params("parallel", "arbitrary"),
        name=name,
    )(x, w)


def _merge_body(om_ref, og_ref, wm_ref, wg_ref, gm_ref, gg_ref, o_ref, wm_bf, wg_bf):
    _cast_resident(wm_ref, wm_bf)
    _cast_resident(wg_ref, wg_bf)
    ym = jnp.dot(om_ref[...], wm_bf[...], preferred_element_type=F32)
    yg = jnp.dot(og_ref[...], wg_bf[...], preferred_element_type=F32)
    merged = (jax.nn.sigmoid(gm_ref[...].astype(F32)) * ym
              + jax.nn.sigmoid(gg_ref[...].astype(F32)) * yg)
    o_ref[...] = merged.astype(o_ref.dtype)


def _merge(o_moba, o_gla, w_moba, w_gla, gates, tm, tn):
    m, km = o_moba.shape
    kg = o_gla.shape[1]
    d = w_moba.shape[1]
    nj = d // tn
    return pl.pallas_call(
        _merge_body,
        grid=(nj, m // tm),
        in_specs=[pl.BlockSpec((tm, km), lambda j, i: (i, 0)),
                  pl.BlockSpec((tm, kg), lambda j, i: (i, 0)),
                  pl.BlockSpec((km, tn), lambda j, i: (0, j)),
                  pl.BlockSpec((kg, tn), lambda j, i: (0, j)),
                  pl.BlockSpec((tm, tn), lambda j, i: (i, j)),
                  pl.BlockSpec((tm, tn), lambda j, i: (i, j + nj))],
        out_specs=pl.BlockSpec((tm, tn), lambda j, i: (i, j)),
        out_shape=jax.ShapeDtypeStruct((m, d), BF16),
        scratch_shapes=[pltpu.VMEM((km, tn), BF16), pltpu.VMEM((kg, tn), BF16)],
        compiler_params=_params("parallel", "arbitrary"),
        name="branch_merge",
    )(o_moba, o_gla, w_moba, w_gla, gates, gates)


def _ffn_up_body(h_ref, wg_ref, wu_ref, o_ref, wg_bf, wu_bf):
    _cast_resident(wg_ref, wg_bf)
    _cast_resident(wu_ref, wu_bf)
    h = h_ref[...]
    a = jnp.dot(h, wg_bf[...], preferred_element_type=F32)
    b = jnp.dot(h, wu_bf[...], preferred_element_type=F32)
    o_ref[...] = (a * jax.nn.sigmoid(a) * b).astype(o_ref.dtype)


def _ffn_up(h, w_gate, w_up, tm, tn):
    m, k = h.shape
    n = w_gate.shape[1]
    wspec = pl.BlockSpec((k, tn), lambda j, i: (0, j))
    return pl.pallas_call(
        _ffn_up_body,
        grid=(pl.cdiv(n, tn), m // tm),
        in_specs=[pl.BlockSpec((tm, k), lambda j, i: (i, 0)), wspec, wspec],
        out_specs=pl.BlockSpec((tm, tn), lambda j, i: (i, j)),
        out_shape=jax.ShapeDtypeStruct((m, n), BF16),
        scratch_shapes=[pltpu.VMEM((k, tn), BF16), pltpu.VMEM((k, tn), BF16)],
        compiler_params=_params("parallel", "arbitrary"),
        name="ffn_up",
    )(h, w_gate, w_up)


def _mm_acc_body(x_ref, w_ref, o_ref, acc_ref):
    kk = pl.program_id(2)

    @pl.when(kk == 0)
    def _():
        acc_ref[...] = jnp.zeros_like(acc_ref)

    acc_ref[...] += jnp.dot(x_ref[...], w_ref[...], preferred_element_type=F32)

    @pl.when(kk == pl.num_programs(2) - 1)
    def _():
        o_ref[...] = acc_ref[...].astype(o_ref.dtype)


def _matmul_ksplit(x, w, tm, tn, tk, out_dtype, name):
    m, k = x.shape
    n = w.shape[1]
    return pl.pallas_call(
        _mm_acc_body,
        grid=(n // tn, m // tm, k // tk),
        in_specs=[pl.BlockSpec((tm, tk), lambda j, i, kk: (i, kk)),
                  pl.BlockSpec((tk, tn), lambda j, i, kk: (kk, j))],
        out_specs=pl.BlockSpec((tm, tn), lambda j, i, kk: (i, j)),
        out_shape=jax.ShapeDtypeStruct((m, n), out_dtype),
        scratch_shapes=[pltpu.VMEM((tm, tn), F32)],
        compiler_params=_params("parallel", "parallel", "arbitrary"),
        name=name,
    )(x, w)


def _moba_body(q_ref, k_ref, v_ref, cos_ref, sin_ref, o_ref, *, nb):
    blk = MOBA_BLOCK
    hd = MOBA_HEAD_DIM
    scale = hd ** -0.5
    neg_inf = -jnp.inf
    cos = cos_ref[...]
    sin = sin_ref[...]

    def rope(t):
        t = t.astype(F32)
        return t * cos + pltpu.roll(t, hd // 2, axis=1) * sin

    qr = rope(q_ref[0])
    kr = rope(k_ref[0])
    q_bf = (qr * scale).astype(BF16)
    k_bf = kr.astype(BF16)
    v_bf = v_ref[0]

    kmean = jnp.mean(kr.reshape(nb, blk, hd), axis=1)
    kmean_rep = jnp.concatenate([kmean] * (LANES // nb), axis=0)
    gate = lax.dot_general(qr, kmean_rep, _NT, precision=lax.Precision.HIGHEST,
                           preferred_element_type=F32)

    lane = lax.broadcasted_iota(jnp.int32, (blk, LANES), 1)
    n_of = lane % nb
    row = lax.broadcasted_iota(jnp.int32, (blk, blk), 0)
    col = lax.broadcasted_iota(jnp.int32, (blk, blk), 1)

    for i in range(nb):
        rows = slice(i * blk, (i + 1) * blk)
        g = jnp.where(n_of < i, gate[rows], neg_inf)
        finite = jnp.abs(g) < jnp.inf
        if i <= MOBA_TOPK:
            bias = jnp.where(finite, 0.0, neg_inf)
        else:
            rank = jnp.zeros((blk, LANES), jnp.int32)
            for j in range(1, nb):
                other = pltpu.roll(g, j, axis=1)
                tie = jnp.where((n_of - j + nb) % nb < n_of, 1, 0)
                rank = rank + jnp.where(other > g, 1, jnp.where(other == g, tie, 0))
            bias = jnp.where(rank < MOBA_TOPK, jnp.where(finite, 0.0, neg_inf), neg_inf)

        kv = slice(0, (i + 1) * blk)
        s = lax.dot_general(q_bf[rows], k_bf[kv], _NT, preferred_element_type=F32)
        pieces = [s[:, n * blk:(n + 1) * blk] + bias[:, n:n + 1] for n in range(i)]
        pieces.append(jnp.where(col <= row, s[:, i * blk:], neg_inf))
        s = jnp.concatenate(pieces, axis=1) if i else pieces[0]
        m = jnp.max(s, axis=1, keepdims=True)
        p = jnp.exp(s - m)
        l = jnp.sum(p, axis=1, keepdims=True)
        o = jnp.dot(p.astype(BF16), v_bf[kv], preferred_element_type=F32) / l
        o_ref[0, rows, :] = o.astype(o_ref.dtype)


def _moba(u, cos, sin, batch, seq, heads):
    hd = MOBA_HEAD_DIM
    nb = seq // MOBA_BLOCK
    blk3 = lambda off: pl.BlockSpec((1, seq, hd), lambda b, h: (b, 0, h + off))
    tab = pl.BlockSpec((seq, hd), lambda b, h: (0, 0))
    return pl.pallas_call(
        functools.partial(_moba_body, nb=nb),
        grid=(batch, heads),
        in_specs=[blk3(0), blk3(heads), blk3(2 * heads), tab, tab],
        out_specs=pl.BlockSpec((1, seq, hd), lambda b, h: (b, 0, h)),
        out_shape=jax.ShapeDtypeStruct((batch, seq, heads * hd), BF16),
        compiler_params=_params("parallel", "parallel"),
        name="moba_attention",
    )(u, u, u, cos, sin)


def _split3(x):
    hi = x.astype(BF16)
    r = x - hi.astype(F32)
    mid = r.astype(BF16)
    lo = (r - mid.astype(F32)).astype(BF16)
    return hi, mid, lo


def _gla_body(q_ref, k_ref, v_ref, gr_ref, ga_ref, up_ref, bias_ref, g_ref, o_ref,
              st_ref, la_ref, *, dk):
    c = GLA_CHUNK
    nbc = GLA_BLOCK_CHUNKS
    mid = nbc // 2
    blk = c * nbc
    scale = dk ** -0.5
    row = lax.broadcasted_iota(jnp.int32, (blk, blk), 0)
    col = lax.broadcasted_iota(jnp.int32, (blk, blk), 1)
    shift = c.bit_length() - 1
    row_c = jnp.right_shift(row, shift)
    col_c = jnp.right_shift(col, shift)
    below = row_c > col_c
    diag = (row_c == col_c) & (col <= row)
    tri = jnp.where(diag, 1.0, 0.0).astype(BF16)

    rank = GLA_GATE_RANK
    g = ga_ref[0]
    g_hi = g.astype(BF16).astype(F32)
    lane = lax.broadcasted_iota(jnp.int32, g.shape, 1)
    lhs = jnp.where((lane >= rank) & (lane < 2 * rank), g - g_hi, g_hi).astype(BF16)
    u = up_ref[...]
    u_hi = u.astype(BF16).astype(F32)
    rhs = jnp.where(lax.broadcasted_iota(jnp.int32, u.shape, 0) >= 2 * rank, u - u_hi,
                    u_hi).astype(BF16)
    z = jnp.dot(lhs, rhs, preferred_element_type=F32) + bias_ref[...]
    la_ref[...] = (jnp.minimum(z, 0.0) - jnp.log(1.0 + jnp.exp(-jnp.abs(z)))) / GLA_GATE_TAU
    st_ref[...] = jnp.zeros_like(st_ref)

    def stack(parts):
        return jnp.concatenate(parts, axis=0).astype(BF16)

    def block_step(bi, carry):
        rows = pl.ds(pl.multiple_of(bi * blk, blk), blk)
        hi, md, lo = _split3(la_ref[rows, :])
        b = (jnp.dot(tri, hi, preferred_element_type=F32)
             + jnp.dot(tri, md, preferred_element_type=F32)
             + jnp.dot(tri, lo, preferred_element_type=F32))
        q = q_ref[0, rows, :].astype(F32) * scale
        k = k_ref[0, rows, :].astype(F32)
        q_dec = q * jnp.exp(b)
        k_inv = k * jnp.exp(-b)
        tot = [b[(j + 1) * c - 1:(j + 1) * c, :] for j in range(nbc)]
        pre = [jnp.zeros_like(tot[0])]
        for j in range(nbc):
            pre.append(pre[-1] + tot[j])
        q_mid, k_mid, q_start, k_end = [], [], [], []
        for j in range(nbc):
            sl = slice(j * c, (j + 1) * c)
            q_mid.append(q_dec[sl] * jnp.exp(pre[j] - pre[mid]))
            k_mid.append(k_inv[sl] * jnp.exp(pre[mid] - pre[j]))
            q_start.append(q_dec[sl] * jnp.exp(pre[j]))
            k_end.append(k[sl] * jnp.exp(tot[j] - b[sl]) * jnp.exp(pre[nbc] - pre[j + 1]))
        a_diag = lax.dot_general(q_dec.astype(BF16), k_inv.astype(BF16), _NT,
                                 preferred_element_type=F32)
        a_off = lax.dot_general(stack(q_mid), stack(k_mid), _NT, preferred_element_type=F32)
        attn = jnp.where(below, a_off, jnp.where(diag, a_diag, 0.0)).astype(BF16)

        v = v_ref[0, rows, :]
        st = st_ref[...]
        o = (jnp.dot(attn, v, preferred_element_type=F32)
             + lax.dot_general(stack(q_start), st.astype(BF16), _NT, preferred_element_type=F32))
        st_ref[...] = st * jnp.exp(pre[nbc]) + lax.dot_general(v, stack(k_end), _TN,
                                                               preferred_element_type=F32)
        gr = gr_ref[0, rows, :].astype(F32)
        y = _rms(o, g_ref[...]) * (gr * jax.nn.sigmoid(gr))
        o_ref[0, rows, :] = y.astype(o_ref.dtype)
        return carry

    lax.fori_loop(0, q_ref.shape[1] // blk, block_step, 0, unroll=2)


def _gla(u, ga, up, bias, g, batch, seq, heads, dk, dv, q_off, k_off, v_off, r_off):
    qk = lambda off: pl.BlockSpec((1, seq, dk), lambda b, h: (b, 0, h + off))
    vv = lambda off: pl.BlockSpec((1, seq, dv), lambda b, h: (b, 0, h + off))
    return pl.pallas_call(
        functools.partial(_gla_body, dk=dk),
        grid=(batch, heads),
        in_specs=[qk(q_off), qk(k_off), vv(v_off), vv(r_off),
                  pl.BlockSpec((1, seq, LANES), lambda b, h: (b, 0, 0)),
                  pl.BlockSpec((LANES, dk), lambda b, h: (0, h)),
                  pl.BlockSpec((1, dk), lambda b, h: (0, h)),
                  pl.BlockSpec((1, dv), lambda b, h: (0, 0))],
        out_specs=pl.BlockSpec((1, seq, dv), lambda b, h: (b, 0, h)),
        out_shape=jax.ShapeDtypeStruct((batch, seq, heads * dv), BF16),
        scratch_shapes=[pltpu.VMEM((dv, dk), F32),
                        pltpu.VMEM((seq, dk), F32)],
        compiler_params=_params("parallel", "parallel"),
        name="gla",
    )(u, u, u, u, ga, up, bias, g)


def _tile(n, pref):
    return pref if n % pref == 0 else n


def _layer(x, pre_mix_g, w_in, gate_up, gate_bias, gla_norm_g, w_moba, w_gla, w_out,
           post_mix_g, pre_ffn_g, w_fg, w_fu, w_fd, post_ffn_g):
    batch, seq, d = x.shape
    t = batch * seq
    half = d // 2
    moba_heads = half // MOBA_HEAD_DIM
    gla_heads = d // 512
    dk = half // gla_heads
    dv = d // gla_heads
    n_main = 9 * half
    n_gate0 = n_main + GLA_GATE_RANK
    d_ff = w_fg.shape[1]

    tm = _tile(t, 1024)
    th = _tile(t, 512)
    tn = _tile(d, 1024)
    tr = _tile(t, 256)
    x2 = x.reshape(t, d)
    vec = lambda g: g.reshape(1, -1)

    w_gates = w_in[:, n_gate0:]
    w_ga = jnp.pad(jnp.tile(w_in[:, n_main:n_gate0], (1, 3)),
                   ((0, 0), (0, LANES - 3 * GLA_GATE_RANK)))
    up_pad = jnp.pad(jnp.tile(gate_up, (3, 1)), ((0, LANES - 3 * GLA_GATE_RANK), (0, 0)))

    hd2 = MOBA_HEAD_DIM // 2
    inv_freq = ROPE_THETA ** (-jnp.arange(hd2, dtype=F32) / hd2)
    ang = jnp.arange(seq, dtype=F32)[:, None] * inv_freq[None, :]
    cos = jnp.concatenate([jnp.cos(ang), jnp.cos(ang)], axis=1)
    sin = jnp.concatenate([-jnp.sin(ang), jnp.sin(ang)], axis=1)

    h = _rmsnorm(x2, vec(pre_mix_g), tr)
    u = _matmul(h, w_in, n_main, th, 1024 if n_main % 1024 == 0 else 512, BF16, "in_proj_main")
    gates = _matmul(h, w_gates, 2 * d, th, tn, BF16, "in_proj_gates")
    ga = _matmul(h, w_ga, LANES, tm, LANES, F32, "in_proj_forget")

    u3 = u.reshape(batch, seq, n_main)
    o_moba = _moba(u3, cos, sin, batch, seq, moba_heads)
    o_gla = _gla(u3, ga.reshape(batch, seq, LANES), up_pad, vec(gate_bias), vec(gla_norm_g),
                 batch, seq, gla_heads, dk, dv,
                 q_off=3 * half // dk, k_off=4 * half // dk, v_off=5 * half // dv,
                 r_off=7 * half // dv)

    merged = _merge(o_moba.reshape(t, half), o_gla.reshape(t, d), w_moba, w_gla, gates, th, 512)
    y = _matmul(merged, w_out, d, th, tn, BF16, "out_proj")
    x1, h2 = _mid_norm(y, x2, vec(post_mix_g), vec(pre_ffn_g), tr)

    a = _ffn_up(h2, w_fg, w_fu, th, 512)
    tk = d_ff // 2 if (d_ff // 2) % LANES == 0 else d_ff
    y2 = _matmul_ksplit(a, w_fd.astype(BF16), tm, 512, tk, BF16, "ffn_down")
    out = _final_norm(y2, x1, vec(post_ffn_g), tr)
    return out.reshape(batch, seq, d)


def kernel(x, pre_mix_norm_g, w_in, gla_gate_up, gla_gate_bias, gla_out_norm_g, w_branch_moba,
           w_branch_gla, w_out, post_mix_norm_g, pre_ffn_norm_g, w_ffn_gate, w_ffn_up, w_ffn_down,
           post_ffn_norm_g):
    for layer in range(w_in.shape[0]):
        x = _layer(x, pre_mix_norm_g[layer], w_in[layer], gla_gate_up[layer], gla_gate_bias[layer],
                   gla_out_norm_g[layer], w_branch_moba[layer], w_branch_gla[layer], w_out[layer],
                   post_mix_norm_g[layer], pre_ffn_norm_g[layer], w_ffn_gate[layer],
                   w_ffn_up[layer], w_ffn_down[layer], post_ffn_norm_g[layer])
    return x
```

```python
import functools

import jax
import jax.numpy as jnp
from jax import lax
from jax.experimental import pallas as pl
from jax.experimental.pallas import tpu as pltpu

F32 = jnp.float32
BF16 = jnp.bfloat16

EPS = 1e-6
MOBA_HEAD_DIM = 128
MOBA_BLOCK = 256
MOBA_TOPK = 3
ROPE_THETA = 10000.0
GLA_GATE_RANK = 16
GLA_GATE_TAU = 16.0
GLA_CHUNK = 64
GLA_BLOCK_CHUNKS = 4
LANES = 128
VMEM_LIMIT = 60 * 1024 * 1024

_NT = (((1,), (1,)), ((), ()))
_TN = (((0,), (0,)), ((), ()))


def _params(*sem):
    return pltpu.CompilerParams(dimension_semantics=sem, vmem_limit_bytes=VMEM_LIMIT)


def _rms(x, g):
    ms = jnp.mean(x * x, axis=-1, keepdims=True)
    return x * lax.rsqrt(ms + EPS) * g


def _rmsnorm_body(x_ref, g_ref, o_ref):
    o_ref[...] = _rms(x_ref[...], g_ref[...]).astype(o_ref.dtype)


def _rmsnorm(x, g, tm):
    m, d = x.shape
    return pl.pallas_call(
        _rmsnorm_body,
        grid=(m // tm,),
        in_specs=[pl.BlockSpec((tm, d), lambda i: (i, 0)),
                  pl.BlockSpec((1, d), lambda i: (0, 0))],
        out_specs=pl.BlockSpec((tm, d), lambda i: (i, 0)),
        out_shape=jax.ShapeDtypeStruct((m, d), BF16),
        compiler_params=_params("parallel"),
        name="rmsnorm",
    )(x, g)


def _mid_norm_body(y_ref, x_ref, g1_ref, g2_ref, x1_ref, h_ref):
    x1 = x_ref[...] + _rms(y_ref[...].astype(F32), g1_ref[...])
    x1_ref[...] = x1
    h_ref[...] = _rms(x1, g2_ref[...]).astype(h_ref.dtype)


def _mid_norm(y, x, g1, g2, tm):
    m, d = x.shape
    row = pl.BlockSpec((tm, d), lambda i: (i, 0))
    vec = pl.BlockSpec((1, d), lambda i: (0, 0))
    return pl.pallas_call(
        _mid_norm_body,
        grid=(m // tm,),
        in_specs=[row, row, vec, vec],
        out_specs=[row, row],
        out_shape=[jax.ShapeDtypeStruct((m, d), F32), jax.ShapeDtypeStruct((m, d), BF16)],
        compiler_params=_params("parallel"),
        name="mid_norm",
    )(y, x, g1, g2)


def _final_norm_body(y_ref, x_ref, g_ref, o_ref):
    o_ref[...] = x_ref[...] + _rms(y_ref[...].astype(F32), g_ref[...])


def _final_norm(y, x, g, tm):
    m, d = x.shape
    row = pl.BlockSpec((tm, d), lambda i: (i, 0))
    return pl.pallas_call(
        _final_norm_body,
        grid=(m // tm,),
        in_specs=[row, row, pl.BlockSpec((1, d), lambda i: (0, 0))],
        out_specs=row,
        out_shape=jax.ShapeDtypeStruct((m, d), F32),
        compiler_params=_params("parallel"),
        name="final_norm",
    )(y, x, g)


def _tile_copy(w_hbm, stage, sem, jj, *, tn, n, col0, transposed, ragged):
    width = n % tn if ragged else tn
    first = col0 + jj * tn
    if transposed:
        return pltpu.make_async_copy(w_hbm.at[pl.ds(first, width), :],
                                     stage.at[pl.ds(0, width), :], sem)
    return pltpu.make_async_copy(w_hbm.at[:, pl.ds(first, width)],
                                 stage.at[:, pl.ds(0, width)], sem)


def _stage_weight(w_hbm, stage, wbf, sem, *, tn, n, col0=0, transposed=False):
    j = pl.program_id(0)
    nj = pl.num_programs(0)
    copy = functools.partial(_tile_copy, w_hbm, stage, sem, tn=tn, n=n, col0=col0,
                             transposed=transposed)

    def for_tile(jj, act):
        if n % tn:
            pl.when(jj < nj - 1)(lambda: act(copy(jj, ragged=False)))
            pl.when(jj == nj - 1)(lambda: act(copy(jj, ragged=True)))
        else:
            act(copy(jj, ragged=False))

    @pl.when(pl.program_id(1) == 0)
    def _():
        pl.when(j == 0)(lambda: for_tile(j, lambda c: c.start()))
        for_tile(j, lambda c: c.wait())
        if transposed:
            slab = min(512, stage.shape[1])
            for c0 in range(0, stage.shape[1], slab):
                wbf[c0:c0 + slab, :] = stage[:, c0:c0 + slab].T.astype(BF16)
        else:
            wbf[...] = stage[...].astype(BF16)
        pl.when(j + 1 < nj)(lambda: for_tile(j + 1, lambda c: c.start()))


def _weight_scratch(k, tn, transposed=False):
    return [pltpu.VMEM((tn, k) if transposed else (k, tn), F32), pltpu.VMEM((k, tn), BF16)]


_HBM = pl.BlockSpec(memory_space=pl.ANY)


def _mm_body(x_ref, w_hbm, o_ref, stage, wbf, sem, **tiles):
    _stage_weight(w_hbm, stage, wbf, sem.at[0], **tiles)
    o_ref[...] = jnp.dot(x_ref[...], wbf[...], preferred_element_type=F32).astype(o_ref.dtype)


def _matmul(x, w, n, tm, tn, out_dtype, name, col0=0, transposed=False):
    m, k = x.shape
    return pl.pallas_call(
        functools.partial(_mm_body, tn=tn, n=n, col0=col0, transposed=transposed),
        grid=(n // tn, m // tm),
        in_specs=[pl.BlockSpec((tm, k), lambda j, i: (i, 0)), _HBM],
        out_specs=pl.BlockSpec((tm, tn), lambda j, i: (i, j)),
        out_shape=jax.ShapeDtypeStruct((m, n), out_dtype),
        scratch_shapes=_weight_scratch(k, tn, transposed) + [pltpu.SemaphoreType.DMA((1,))],
        compiler_params=_params("arbitrary", "arbitrary"),
        name=name,
    )(x, w)


def _mm_small_body(x_ref, w_ref, o_ref):
    o_ref[...] = jnp.dot(x_ref[...], w_ref[...].astype(BF16),
                         preferred_element_type=F32).astype(o_ref.dtype)


def _matmul_small(x, w, tm, out_dtype, name):
    m, k = x.shape
    n = w.shape[1]
    return pl.pallas_call(
        _mm_small_body,
        grid=(m // tm,),
        in_specs=[pl.BlockSpec((tm, k), lambda i: (i, 0)), pl.BlockSpec((k, n), lambda i: (0, 0))],
        out_specs=pl.BlockSpec((tm, n), lambda i: (i, 0)),
        out_shape=jax.ShapeDtypeStruct((m, n), out_dtype),
        compiler_params=_params("parallel"),
        name=name,
    )(x, w)


def _merge_body(om_ref, og_ref, wm_hbm, wg_hbm, gm_ref, gg_ref, o_ref,
                wm_stage, wm_bf, wg_stage, wg_bf, sem, **tiles):
    _stage_weight(wm_hbm, wm_stage, wm_bf, sem.at[0], **tiles)
    _stage_weight(wg_hbm, wg_stage, wg_bf, sem.at[1], **tiles)
    ym = jnp.dot(om_ref[...], wm_bf[...], preferred_element_type=F32)
    yg = jnp.dot(og_ref[...], wg_bf[...], preferred_element_type=F32)
    merged = (jax.nn.sigmoid(gm_ref[...].astype(F32)) * ym
              + jax.nn.sigmoid(gg_ref[...].astype(F32)) * yg)
    o_ref[...] = merged.astype(o_ref.dtype)


def _merge(o_moba, o_gla, w_moba, w_gla, gates, tm, tn):
    m, km = o_moba.shape
    kg = o_gla.shape[1]
    d = w_moba.shape[1]
    nj = d // tn
    return pl.pallas_call(
        functools.partial(_merge_body, tn=tn, n=d),
        grid=(nj, m // tm),
        in_specs=[pl.BlockSpec((tm, km), lambda j, i: (i, 0)),
                  pl.BlockSpec((tm, kg), lambda j, i: (i, 0)),
                  _HBM, _HBM,
                  pl.BlockSpec((tm, tn), lambda j, i: (i, j)),
                  pl.BlockSpec((tm, tn), lambda j, i: (i, j + nj))],
        out_specs=pl.BlockSpec((tm, tn), lambda j, i: (i, j)),
        out_shape=jax.ShapeDtypeStruct((m, d), BF16),
        scratch_shapes=(_weight_scratch(km, tn) + _weight_scratch(kg, tn)
                        + [pltpu.SemaphoreType.DMA((2,))]),
        compiler_params=_params("arbitrary", "arbitrary"),
        name="branch_merge",
    )(o_moba, o_gla, w_moba, w_gla, gates, gates)


def _ffn_up_body(h_ref, wg_hbm, wu_hbm, o_ref, wg_stage, wg_bf, wu_stage, wu_bf, sem, **tiles):
    _stage_weight(wg_hbm, wg_stage, wg_bf, sem.at[0], **tiles)
    _stage_weight(wu_hbm, wu_stage, wu_bf, sem.at[1], **tiles)
    h = h_ref[...]
    a = jnp.dot(h, wg_bf[...], preferred_element_type=F32)
    b = jnp.dot(h, wu_bf[...], preferred_element_type=F32)
    o_ref[...] = (a * jax.nn.sigmoid(a) * b).astype(o_ref.dtype)


def _ffn_up(h, w_gate, w_up, tm, tn):
    m, k = h.shape
    n = w_gate.shape[1]
    return pl.pallas_call(
        functools.partial(_ffn_up_body, tn=tn, n=n),
        grid=(pl.cdiv(n, tn), m // tm),
        in_specs=[pl.BlockSpec((tm, k), lambda j, i: (i, 0)), _HBM, _HBM],
        out_specs=pl.BlockSpec((tm, tn), lambda j, i: (i, j)),
        out_shape=jax.ShapeDtypeStruct((m, n), BF16),
        scratch_shapes=(_weight_scratch(k, tn) + _weight_scratch(k, tn)
                        + [pltpu.SemaphoreType.DMA((2,))]),
        compiler_params=_params("arbitrary", "arbitrary"),
        name="ffn_up",
    )(h, w_gate, w_up)


def _mm_acc_body(x_ref, w_ref, o_ref, acc_ref):
    kk = pl.program_id(2)

    @pl.when(kk == 0)
    def _():
        acc_ref[...] = jnp.zeros_like(acc_ref)

    acc_ref[...] += jnp.dot(x_ref[...], w_ref[...], preferred_element_type=F32)

    @pl.when(kk == pl.num_programs(2) - 1)
    def _():
        o_ref[...] = acc_ref[...].astype(o_ref.dtype)


def _matmul_ksplit(x, w, tm, tn, tk, out_dtype, name):
    m, k = x.shape
    n = w.shape[1]
    return pl.pallas_call(
        _mm_acc_body,
        grid=(n // tn, m // tm, k // tk),
        in_specs=[pl.BlockSpec((tm, tk), lambda j, i, kk: (i, kk)),
                  pl.BlockSpec((tk, tn), lambda j, i, kk: (kk, j))],
        out_specs=pl.BlockSpec((tm, tn), lambda j, i, kk: (i, j)),
        out_shape=jax.ShapeDtypeStruct((m, n), out_dtype),
        scratch_shapes=[pltpu.VMEM((tm, tn), F32)],
        compiler_params=_params("parallel", "parallel", "arbitrary"),
        name=name,
    )(x, w)


def _moba_body(q_ref, k_ref, v_ref, cos_ref, sin_ref, o_ref, *, nb):
    blk = MOBA_BLOCK
    hd = MOBA_HEAD_DIM
    scale = hd ** -0.5
    neg_inf = -jnp.inf
    cos = cos_ref[...]
    sin = sin_ref[...]

    def rope(t):
        t = t.astype(F32)
        return t * cos + pltpu.roll(t, hd // 2, axis=1) * sin

    qr = rope(q_ref[0])
    kr = rope(k_ref[0])
    q_bf = (qr * scale).astype(BF16)
    k_bf = kr.astype(BF16)
    v_bf = v_ref[0]

    kmean = jnp.mean(kr.reshape(nb, blk, hd), axis=1)
    kmean_rep = jnp.concatenate([kmean] * (LANES // nb), axis=0)
    gate = lax.dot_general(qr, kmean_rep, _NT, precision=lax.Precision.HIGHEST,
                           preferred_element_type=F32)

    lane = lax.broadcasted_iota(jnp.int32, (blk, LANES), 1)
    n_of = lane % nb
    row = lax.broadcasted_iota(jnp.int32, (blk, blk), 0)
    col = lax.broadcasted_iota(jnp.int32, (blk, blk), 1)

    for i in range(nb):
        rows = slice(i * blk, (i + 1) * blk)
        g = jnp.where(n_of < i, gate[rows], neg_inf)
        finite = jnp.abs(g) < jnp.inf
        if i <= MOBA_TOPK:
            bias = jnp.where(finite, 0.0, neg_inf)
        else:
            rank = jnp.zeros((blk, LANES), jnp.int32)
            for j in range(1, nb):
                other = pltpu.roll(g, j, axis=1)
                tie = jnp.where((n_of - j + nb) % nb < n_of, 1, 0)
                rank = rank + jnp.where(other > g, 1, jnp.where(other == g, tie, 0))
            bias = jnp.where(rank < MOBA_TOPK, jnp.where(finite, 0.0, neg_inf), neg_inf)

        kv = slice(0, (i + 1) * blk)
        s = lax.dot_general(q_bf[rows], k_bf[kv], _NT, preferred_element_type=F32)
        pieces = [s[:, n * blk:(n + 1) * blk] + bias[:, n:n + 1] for n in range(i)]
        pieces.append(jnp.where(col <= row, s[:, i * blk:], neg_inf))
        s = jnp.concatenate(pieces, axis=1) if i else pieces[0]
        m = jnp.max(s, axis=1, keepdims=True)
        p = jnp.exp(s - m)
        l = jnp.sum(p, axis=1, keepdims=True)
        o = jnp.dot(p.astype(BF16), v_bf[kv], preferred_element_type=F32) / l
        o_ref[0, rows, :] = o.astype(o_ref.dtype)


def _moba(u, cos, sin, batch, seq, heads):
    hd = MOBA_HEAD_DIM
    nb = seq // MOBA_BLOCK
    blk3 = lambda off: pl.BlockSpec((1, seq, hd), lambda b, h: (b, 0, h + off))
    tab = pl.BlockSpec((seq, hd), lambda b, h: (0, 0))
    return pl.pallas_call(
        functools.partial(_moba_body, nb=nb),
        grid=(batch, heads),
        in_specs=[blk3(0), blk3(heads), blk3(2 * heads), tab, tab],
        out_specs=pl.BlockSpec((1, seq, hd), lambda b, h: (b, 0, h)),
        out_shape=jax.ShapeDtypeStruct((batch, seq, heads * hd), BF16),
        compiler_params=_params("parallel", "parallel"),
        name="moba_attention",
    )(u, u, u, cos, sin)


def _split3(x):
    hi = x.astype(BF16)
    r = x - hi.astype(F32)
    mid = r.astype(BF16)
    lo = (r - mid.astype(F32)).astype(BF16)
    return hi, mid, lo


def _gla_body(q_ref, k_ref, v_ref, gr_ref, ga_ref, up_ref, bias_ref, g_ref, o_ref,
              st_ref, la_ref, *, dk):
    c = GLA_CHUNK
    nbc = GLA_BLOCK_CHUNKS
    mid = nbc // 2
    blk = c * nbc
    scale = dk ** -0.5
    row = lax.broadcasted_iota(jnp.int32, (blk, blk), 0)
    col = lax.broadcasted_iota(jnp.int32, (blk, blk), 1)
    shift = c.bit_length() - 1
    row_c = jnp.right_shift(row, shift)
    col_c = jnp.right_shift(col, shift)
    below = row_c > col_c
    diag = (row_c == col_c) & (col <= row)
    tri = jnp.where(diag, 1.0, 0.0).astype(BF16)

    rank = GLA_GATE_RANK
    g = ga_ref[0]
    g_hi = g.astype(BF16).astype(F32)
    lane = lax.broadcasted_iota(jnp.int32, g.shape, 1)
    lhs = jnp.where((lane >= rank) & (lane < 2 * rank), g - g_hi, g_hi).astype(BF16)
    u = up_ref[...]
    u_hi = u.astype(BF16).astype(F32)
    rhs = jnp.where(lax.broadcasted_iota(jnp.int32, u.shape, 0) >= 2 * rank, u - u_hi,
                    u_hi).astype(BF16)
    z = jnp.dot(lhs, rhs, preferred_element_type=F32) + bias_ref[...]
    la_ref[...] = (jnp.minimum(z, 0.0) - jnp.log(1.0 + jnp.exp(-jnp.abs(z)))) / GLA_GATE_TAU
    st_ref[...] = jnp.zeros_like(st_ref)

    def stack(parts):
        return jnp.concatenate(parts, axis=0).astype(BF16)

    def block_step(bi, carry):
        rows = pl.ds(pl.multiple_of(bi * blk, blk), blk)
        hi, md, lo = _split3(la_ref[rows, :])
        b = (jnp.dot(tri, hi, preferred_element_type=F32)
             + jnp.dot(tri, md, preferred_element_type=F32)
             + jnp.dot(tri, lo, preferred_element_type=F32))
        q = q_ref[0, rows, :].astype(F32) * scale
        k = k_ref[0, rows, :].astype(F32)
        q_dec = q * jnp.exp(b)
        k_inv = k * jnp.exp(-b)
        tot = [b[(j + 1) * c - 1:(j + 1) * c, :] for j in range(nbc)]
        pre = [jnp.zeros_like(tot[0])]
        for j in range(nbc):
            pre.append(pre[-1] + tot[j])
        q_mid, k_mid, q_start, k_end = [], [], [], []
        for j in range(nbc):
            sl = slice(j * c, (j + 1) * c)
            q_mid.append(q_dec[sl] * jnp.exp(pre[j] - pre[mid]))
            k_mid.append(k_inv[sl] * jnp.exp(pre[mid] - pre[j]))
            q_start.append(q_dec[sl] * jnp.exp(pre[j]))
            k_end.append(k[sl] * jnp.exp(tot[j] - b[sl]) * jnp.exp(pre[nbc] - pre[j + 1]))
        a_diag = lax.dot_general(q_dec.astype(BF16), k_inv.astype(BF16), _NT,
                                 preferred_element_type=F32)
        a_off = lax.dot_general(stack(q_mid), stack(k_mid), _NT, preferred_element_type=F32)
        attn = jnp.where(below, a_off, jnp.where(diag, a_diag, 0.0)).astype(BF16)

        v = v_ref[0, rows, :]
        st = st_ref[...]
        o = (jnp.dot(attn, v, preferred_element_type=F32)
             + lax.dot_general(stack(q_start), st.astype(BF16), _NT, preferred_element_type=F32))
        st_ref[...] = st * jnp.exp(pre[nbc]) + lax.dot_general(v, stack(k_end), _TN,
                                                               preferred_element_type=F32)
        gr = gr_ref[0, rows, :].astype(F32)
        y = _rms(o, g_ref[...]) * (gr * jax.nn.sigmoid(gr))
        o_ref[0, rows, :] = y.astype(o_ref.dtype)
        return carry

    lax.fori_loop(0, q_ref.shape[1] // blk, block_step, 0, unroll=2)


def _gla(u, ga, up, bias, g, batch, seq, heads, dk, dv, q_off, k_off, v_off, r_off):
    qk = lambda off: pl.BlockSpec((1, seq, dk), lambda b, h: (b, 0, h + off))
    vv = lambda off: pl.BlockSpec((1, seq, dv), lambda b, h: (b, 0, h + off))
    return pl.pallas_call(
        functools.partial(_gla_body, dk=dk),
        grid=(batch, heads),
        in_specs=[qk(q_off), qk(k_off), vv(v_off), vv(r_off),
                  pl.BlockSpec((1, seq, LANES), lambda b, h: (b, 0, 0)),
                  pl.BlockSpec((LANES, dk), lambda b, h: (0, h)),
                  pl.BlockSpec((1, dk), lambda b, h: (0, h)),
                  pl.BlockSpec((1, dv), lambda b, h: (0, 0))],
        out_specs=pl.BlockSpec((1, seq, dv), lambda b, h: (b, 0, h)),
        out_shape=jax.ShapeDtypeStruct((batch, seq, heads * dv), BF16),
        scratch_shapes=[pltpu.VMEM((dv, dk), F32),
                        pltpu.VMEM((seq, dk), F32)],
        compiler_params=_params("parallel", "parallel"),
        name="gla",
    )(u, u, u, u, ga, up, bias, g)


def _tile(n, pref):
    return pref if n % pref == 0 else n


def _layer(x, pre_mix_g, w_in, gate_up, gate_bias, gla_norm_g, w_moba, w_gla, w_out,
           post_mix_g, pre_ffn_g, w_fg, w_fu, w_fd, post_ffn_g):
    batch, seq, d = x.shape
    t = batch * seq
    half = d // 2
    moba_heads = half // MOBA_HEAD_DIM
    gla_heads = d // 512
    dk = half // gla_heads
    dv = d // gla_heads
    n_main = 9 * half
    n_gate0 = n_main + GLA_GATE_RANK
    d_ff = w_fg.shape[1]

    tm = _tile(t, 1024)
    tn = _tile(d, 1024)
    tr = _tile(t, 256)
    x2 = x.reshape(t, d)
    vec = lambda g: g.reshape(1, -1)

    w_in_t = w_in.T
    w_ga = jnp.pad(jnp.tile(w_in_t[n_main:n_gate0], (3, 1)),
                   ((0, LANES - 3 * GLA_GATE_RANK), (0, 0))).T
    up_pad = jnp.pad(jnp.tile(gate_up, (3, 1)), ((0, LANES - 3 * GLA_GATE_RANK), (0, 0)))

    hd2 = MOBA_HEAD_DIM // 2
    inv_freq = ROPE_THETA ** (-jnp.arange(hd2, dtype=F32) / hd2)
    ang = jnp.arange(seq, dtype=F32)[:, None] * inv_freq[None, :]
    cos = jnp.concatenate([jnp.cos(ang), jnp.cos(ang)], axis=1)
    sin = jnp.concatenate([-jnp.sin(ang), jnp.sin(ang)], axis=1)

    h = _rmsnorm(x2, vec(pre_mix_g), tr)
    u = _matmul(h, w_in_t, n_main, tm, 1024 if n_main % 1024 == 0 else 512, BF16, "in_proj_main",
                transposed=True)
    gates = _matmul(h, w_in_t, 2 * d, tm, tn, BF16, "in_proj_gates", col0=n_gate0, transposed=True)
    ga = _matmul_small(h, w_ga, tm, F32, "in_proj_forget")

    u3 = u.reshape(batch, seq, n_main)
    o_moba = _moba(u3, cos, sin, batch, seq, moba_heads)
    o_gla = _gla(u3, ga.reshape(batch, seq, LANES), up_pad, vec(gate_bias), vec(gla_norm_g),
                 batch, seq, gla_heads, dk, dv,
                 q_off=3 * half // dk, k_off=4 * half // dk, v_off=5 * half // dv,
                 r_off=7 * half // dv)

    merged = _merge(o_moba.reshape(t, half), o_gla.reshape(t, d), w_moba, w_gla, gates, tm, 512)
    y = _matmul(merged, w_out, d, tm, tn, BF16, "out_proj")
    x1, h2 = _mid_norm(y, x2, vec(post_mix_g), vec(pre_ffn_g), tr)

    a = _ffn_up(h2, w_fg, w_fu, tm, 512)
    tk = d_ff // 2 if (d_ff // 2) % LANES == 0 else d_ff
    y2 = _matmul_ksplit(a, w_fd.astype(BF16), tm, 512, tk, BF16, "ffn_down")
    out = _final_norm(y2, x1, vec(post_ffn_g), tr)
    return out.reshape(batch, seq, d)


def kernel(x, pre_mix_norm_g, w_in, gla_gate_up, gla_gate_bias, gla_out_norm_g, w_branch_moba,
           w_branch_gla, w_out, post_mix_norm_g, pre_ffn_norm_g, w_ffn_gate, w_ffn_up, w_ffn_down,
           post_ffn_norm_g):
    for layer in range(w_in.shape[0]):
        x = _layer(x, pre_mix_norm_g[layer], w_in[layer], gla_gate_up[layer], gla_gate_bias[layer],
                   gla_out_norm_g[layer], w_branch_moba[layer], w_branch_gla[layer], w_out[layer],
                   post_mix_norm_g[layer], pre_ffn_norm_g[layer], w_ffn_gate[layer],
                   w_ffn_up[layer], w_ffn_down[layer], post_ffn_norm_g[layer])
    return x
```

```python
import functools

import jax
import jax.numpy as jnp
from jax import lax
from jax.experimental import pallas as pl
from jax.experimental.pallas import tpu as pltpu

F32 = jnp.float32
BF16 = jnp.bfloat16

EPS = 1e-6
MOBA_HEAD_DIM = 128
MOBA_BLOCK = 256
MOBA_TOPK = 3
MOBA_HEADS_PER_STEP = 2
ROPE_THETA = 10000.0
LOG2_E = 1.4426950408889634
MASKED = -1e30
GLA_GATE_RANK = 16
GLA_GATE_TAU = 16.0
GLA_CHUNK = 64
GLA_BLOCK_CHUNKS = 4
LANES = 128
VMEM_LIMIT = 60 * 1024 * 1024

_NT = (((1,), (1,)), ((), ()))
_TN = (((0,), (0,)), ((), ()))


def _params(*sem):
    return pltpu.CompilerParams(dimension_semantics=sem, vmem_limit_bytes=VMEM_LIMIT)


def _rms(x, g):
    ms = jnp.mean(x * x, axis=-1, keepdims=True)
    return x * lax.rsqrt(ms + EPS) * g


def _rmsnorm_body(x_ref, g_ref, o_ref):
    o_ref[...] = _rms(x_ref[...], g_ref[...]).astype(o_ref.dtype)


def _rmsnorm(x, g, tm):
    m, d = x.shape
    return pl.pallas_call(
        _rmsnorm_body,
        grid=(m // tm,),
        in_specs=[pl.BlockSpec((tm, d), lambda i: (i, 0)),
                  pl.BlockSpec((1, d), lambda i: (0, 0))],
        out_specs=pl.BlockSpec((tm, d), lambda i: (i, 0)),
        out_shape=jax.ShapeDtypeStruct((m, d), BF16),
        compiler_params=_params("parallel"),
        name="rmsnorm",
    )(x, g)


def _mid_norm_body(y_ref, x_ref, g1_ref, g2_ref, x1_ref, h_ref):
    x1 = x_ref[...] + _rms(y_ref[...].astype(F32), g1_ref[...])
    x1_ref[...] = x1
    h_ref[...] = _rms(x1, g2_ref[...]).astype(h_ref.dtype)


def _mid_norm(y, x, g1, g2, tm):
    m, d = x.shape
    row = pl.BlockSpec((tm, d), lambda i: (i, 0))
    vec = pl.BlockSpec((1, d), lambda i: (0, 0))
    return pl.pallas_call(
        _mid_norm_body,
        grid=(m // tm,),
        in_specs=[row, row, vec, vec],
        out_specs=[row, row],
        out_shape=[jax.ShapeDtypeStruct((m, d), F32), jax.ShapeDtypeStruct((m, d), BF16)],
        compiler_params=_params("parallel"),
        name="mid_norm",
    )(y, x, g1, g2)


def _final_norm_body(y_ref, x_ref, g_ref, o_ref):
    o_ref[...] = x_ref[...] + _rms(y_ref[...].astype(F32), g_ref[...])


def _final_norm(y, x, g, tm):
    m, d = x.shape
    row = pl.BlockSpec((tm, d), lambda i: (i, 0))
    return pl.pallas_call(
        _final_norm_body,
        grid=(m // tm,),
        in_specs=[row, row, pl.BlockSpec((1, d), lambda i: (0, 0))],
        out_specs=row,
        out_shape=jax.ShapeDtypeStruct((m, d), F32),
        compiler_params=_params("parallel"),
        name="final_norm",
    )(y, x, g)


def _tile_copy(w_hbm, stage, sem, jj, *, tn, n, col0, transposed, ragged):
    width = n % tn if ragged else tn
    first = col0 + jj * tn
    if transposed:
        return pltpu.make_async_copy(w_hbm.at[pl.ds(first, width), :],
                                     stage.at[pl.ds(0, width), :], sem)
    return pltpu.make_async_copy(w_hbm.at[:, pl.ds(first, width)],
                                 stage.at[:, pl.ds(0, width)], sem)


def _stage_weight(w_hbm, stage, wbf, sem, *, tn, n, col0=0, transposed=False):
    j = pl.program_id(0)
    nj = pl.num_programs(0)
    copy = functools.partial(_tile_copy, w_hbm, stage, sem, tn=tn, n=n, col0=col0,
                             transposed=transposed)

    def for_tile(jj, act):
        if n % tn:
            pl.when(jj < nj - 1)(lambda: act(copy(jj, ragged=False)))
            pl.when(jj == nj - 1)(lambda: act(copy(jj, ragged=True)))
        else:
            act(copy(jj, ragged=False))

    @pl.when(pl.program_id(1) == 0)
    def _():
        pl.when(j == 0)(lambda: for_tile(j, lambda c: c.start()))
        for_tile(j, lambda c: c.wait())
        if transposed:
            slab = min(512, stage.shape[1])
            for c0 in range(0, stage.shape[1], slab):
                wbf[c0:c0 + slab, :] = stage[:, c0:c0 + slab].T.astype(BF16)
        else:
            wbf[...] = stage[...].astype(BF16)
        pl.when(j + 1 < nj)(lambda: for_tile(j + 1, lambda c: c.start()))


def _weight_scratch(k, tn, transposed=False):
    return [pltpu.VMEM((tn, k) if transposed else (k, tn), F32), pltpu.VMEM((k, tn), BF16)]


_HBM = pl.BlockSpec(memory_space=pl.ANY)


def _mm_body(x_ref, w_hbm, o_ref, stage, wbf, sem, **tiles):
    _stage_weight(w_hbm, stage, wbf, sem.at[0], **tiles)
    o_ref[...] = jnp.dot(x_ref[...], wbf[...], preferred_element_type=F32).astype(o_ref.dtype)


def _matmul(x, w, n, tm, tn, out_dtype, name, col0=0, transposed=False):
    m, k = x.shape
    return pl.pallas_call(
        functools.partial(_mm_body, tn=tn, n=n, col0=col0, transposed=transposed),
        grid=(n // tn, m // tm),
        in_specs=[pl.BlockSpec((tm, k), lambda j, i: (i, 0)), _HBM],
        out_specs=pl.BlockSpec((tm, tn), lambda j, i: (i, j)),
        out_shape=jax.ShapeDtypeStruct((m, n), out_dtype),
        scratch_shapes=_weight_scratch(k, tn, transposed) + [pltpu.SemaphoreType.DMA((1,))],
        compiler_params=_params("arbitrary", "arbitrary"),
        name=name,
    )(x, w)


def _mm_small_body(x_ref, w_ref, o_ref):
    o_ref[...] = jnp.dot(x_ref[...], w_ref[...].astype(BF16),
                         preferred_element_type=F32).astype(o_ref.dtype)


def _matmul_small(x, w, tm, out_dtype, name):
    m, k = x.shape
    n = w.shape[1]
    return pl.pallas_call(
        _mm_small_body,
        grid=(m // tm,),
        in_specs=[pl.BlockSpec((tm, k), lambda i: (i, 0)), pl.BlockSpec((k, n), lambda i: (0, 0))],
        out_specs=pl.BlockSpec((tm, n), lambda i: (i, 0)),
        out_shape=jax.ShapeDtypeStruct((m, n), out_dtype),
        compiler_params=_params("parallel"),
        name=name,
    )(x, w)


def _merge_body(om_ref, og_ref, wm_hbm, wg_hbm, gm_ref, gg_ref, o_ref,
                wm_stage, wm_bf, wg_stage, wg_bf, sem, **tiles):
    _stage_weight(wm_hbm, wm_stage, wm_bf, sem.at[0], **tiles)
    _stage_weight(wg_hbm, wg_stage, wg_bf, sem.at[1], **tiles)
    ym = jnp.dot(om_ref[...], wm_bf[...], preferred_element_type=F32)
    yg = jnp.dot(og_ref[...], wg_bf[...], preferred_element_type=F32)
    merged = (jax.nn.sigmoid(gm_ref[...].astype(F32)) * ym
              + jax.nn.sigmoid(gg_ref[...].astype(F32)) * yg)
    o_ref[...] = merged.astype(o_ref.dtype)


def _merge(o_moba, o_gla, w_moba, w_gla, gates, tm, tn):
    m, km = o_moba.shape
    kg = o_gla.shape[1]
    d = w_moba.shape[1]
    nj = d // tn
    return pl.pallas_call(
        functools.partial(_merge_body, tn=tn, n=d),
        grid=(nj, m // tm),
        in_specs=[pl.BlockSpec((tm, km), lambda j, i: (i, 0)),
                  pl.BlockSpec((tm, kg), lambda j, i: (i, 0)),
                  _HBM, _HBM,
                  pl.BlockSpec((tm, tn), lambda j, i: (i, j)),
                  pl.BlockSpec((tm, tn), lambda j, i: (i, j + nj))],
        out_specs=pl.BlockSpec((tm, tn), lambda j, i: (i, j)),
        out_shape=jax.ShapeDtypeStruct((m, d), BF16),
        scratch_shapes=(_weight_scratch(km, tn) + _weight_scratch(kg, tn)
                        + [pltpu.SemaphoreType.DMA((2,))]),
        compiler_params=_params("arbitrary", "arbitrary"),
        name="branch_merge",
    )(o_moba, o_gla, w_moba, w_gla, gates, gates)


def _ffn_up_body(h_ref, wg_hbm, wu_hbm, wd_ref, o_ref, wd_bf_ref,
                 wg_stage, wg_bf, wu_stage, wu_bf, sem, **tiles):
    _stage_weight(wg_hbm, wg_stage, wg_bf, sem.at[0], **tiles)
    _stage_weight(wu_hbm, wu_stage, wu_bf, sem.at[1], **tiles)
    h = h_ref[...]
    a = jnp.dot(h, wg_bf[...], preferred_element_type=F32)
    b = jnp.dot(h, wu_bf[...], preferred_element_type=F32)
    o_ref[...] = (a * jax.nn.sigmoid(a) * b).astype(o_ref.dtype)
    wd_bf_ref[...] = wd_ref[...].astype(BF16)


def _ffn_up(h, w_gate, w_up, w_down, tm, tn):
    m, k = h.shape
    n = w_gate.shape[1]
    ni = m // tm
    steps = pl.cdiv(n, tn) * ni
    slab = -(-pl.cdiv(w_down.shape[0], steps) // 16) * 16
    last = pl.cdiv(w_down.shape[0], slab) - 1
    wd_spec = pl.BlockSpec((slab, w_down.shape[1]), lambda j, i: (jnp.minimum(j * ni + i, last), 0))
    return pl.pallas_call(
        functools.partial(_ffn_up_body, tn=tn, n=n),
        grid=(pl.cdiv(n, tn), ni),
        in_specs=[pl.BlockSpec((tm, k), lambda j, i: (i, 0)), _HBM, _HBM, wd_spec],
        out_specs=[pl.BlockSpec((tm, tn), lambda j, i: (i, j)), wd_spec],
        out_shape=[jax.ShapeDtypeStruct((m, n), BF16), jax.ShapeDtypeStruct(w_down.shape, BF16)],
        scratch_shapes=(_weight_scratch(k, tn) + _weight_scratch(k, tn)
                        + [pltpu.SemaphoreType.DMA((2,))]),
        compiler_params=_params("arbitrary", "arbitrary"),
        name="ffn_up",
    )(h, w_gate, w_up, w_down)


def _mm_acc_body(x_ref, w_ref, o_ref, acc_ref):
    kk = pl.program_id(2)

    @pl.when(kk == 0)
    def _():
        acc_ref[...] = jnp.zeros_like(acc_ref)

    acc_ref[...] += jnp.dot(x_ref[...], w_ref[...], preferred_element_type=F32)

    @pl.when(kk == pl.num_programs(2) - 1)
    def _():
        o_ref[...] = acc_ref[...].astype(o_ref.dtype)


def _matmul_ksplit(x, w, tm, tn, tk, out_dtype, name):
    m, k = x.shape
    n = w.shape[1]
    return pl.pallas_call(
        _mm_acc_body,
        grid=(n // tn, m // tm, k // tk),
        in_specs=[pl.BlockSpec((tm, tk), lambda j, i, kk: (i, kk)),
                  pl.BlockSpec((tk, tn), lambda j, i, kk: (kk, j))],
        out_specs=pl.BlockSpec((tm, tn), lambda j, i, kk: (i, j)),
        out_shape=jax.ShapeDtypeStruct((m, n), out_dtype),
        scratch_shapes=[pltpu.VMEM((tm, tn), F32)],
        compiler_params=_params("parallel", "parallel", "arbitrary"),
        name=name,
    )(x, w)


def _moba_body(q_ref, k_ref, v_ref, cos_ref, sin_ref, o_ref, *, nb, heads):
    blk = MOBA_BLOCK
    hd = MOBA_HEAD_DIM
    seq = q_ref.shape[1]
    scale = hd ** -0.5
    cos = cos_ref[...]
    sin = sin_ref[...]
    row = lax.broadcasted_iota(jnp.int32, (blk, blk), 0)
    col = lax.broadcasted_iota(jnp.int32, (blk, blk), 1)
    causal = col <= row

    r_l = lax.broadcasted_iota(jnp.int32, (LANES, LANES), 0)
    c_l = lax.broadcasted_iota(jnp.int32, (LANES, LANES), 1)
    sum_wins = jnp.where((c_l == r_l % nb) & (r_l < nb * nb), 1.0, 0.0).astype(BF16)
    key_lane = lax.broadcasted_iota(jnp.int32, (seq, LANES), 1)
    key_block = lax.broadcasted_iota(jnp.int32, (seq, LANES), 0) // blk
    block_ind = jnp.where(key_lane == key_block, 1.0, 0.0).astype(BF16)
    ones_col = jnp.where(key_lane == 0, 1.0, 0.0).astype(BF16)

    lane = lax.broadcasted_iota(jnp.int32, (blk, LANES), 1)
    lane_j = lane // nb
    lane_n = lane % nb
    lane_m = (lane_n - lane_j + nb) % nb
    tie = jnp.where(lane_m < lane_n, 1.0, 0.0)

    def rope(t):
        t = t.astype(F32)
        return t * cos + pltpu.roll(t, hd // 2, axis=1) * sin

    def prepare(h):
        cols = slice(h * hd, (h + 1) * hd)
        qr = rope(q_ref[0, :, cols])
        kr = rope(k_ref[0, :, cols])
        kmean = jnp.mean(kr.reshape(nb, blk, hd), axis=1)
        parts = [kmean] + [jnp.roll(kmean, j, axis=0) - kmean for j in range(1, nb)]
        parts.append(jnp.zeros((LANES - nb * nb, hd), F32))
        diff = jnp.concatenate(parts, axis=0)
        q_hi = qr.astype(BF16)
        q_lo = (qr - q_hi.astype(F32)).astype(BF16)
        d_hi = diff.astype(BF16)
        d_lo = (diff - d_hi.astype(F32)).astype(BF16)
        gate = lax.dot_general(jnp.concatenate([q_hi, q_lo, q_hi], axis=1),
                               jnp.concatenate([d_hi, d_hi, d_lo], axis=1), _NT,
                               preferred_element_type=F32)
        return dict(
            cols=cols, gate=gate,
            q=(qr * (scale * LOG2_E)).astype(BF16),
            k=jnp.concatenate([kr.astype(BF16), block_ind], axis=1),
            v=jnp.concatenate([v_ref[0, :, cols], ones_col], axis=1))

    def block_bias(gate, i):
        finite = jnp.abs(gate) < jnp.inf
        if i <= MOBA_TOPK:
            keep = finite
        else:
            compared = (lane_j >= 1) & (lane_j < nb) & (lane_n < i) & (lane_m < i)
            wins = jnp.where(compared, jnp.where(gate > 0.0, 1.0, jnp.where(gate == 0.0, tie, 0.0)), 0.0)
            wins = jnp.where(lane < nb, jnp.where(finite, 0.0, float(MOBA_TOPK)), wins)
            rank = jnp.dot(wins.astype(BF16), sum_wins, preferred_element_type=F32)
            keep = rank < MOBA_TOPK
        return jnp.where(lane < i, jnp.where(keep, 0.0, MASKED), jnp.where(lane == i, 0.0, MASKED))

    def probabilities(s, i):
        own = jnp.where(causal, s[:, i * blk:], -jnp.inf)
        s = jnp.concatenate([s[:, :i * blk], own], axis=1) if i else own
        m = jnp.max(s, axis=1, keepdims=True)
        return jnp.exp2(s - m).astype(BF16)

    hs = [prepare(h) for h in range(heads)]
    for i in range(nb):
        rows = slice(i * blk, (i + 1) * blk)
        kv = slice(0, (i + 1) * blk)
        q_aug = [jnp.concatenate([h["q"][rows], block_bias(h["gate"][rows], i).astype(BF16)], axis=1)
                 for h in hs]
        s = [lax.dot_general(q_aug[n], h["k"][kv], _NT, preferred_element_type=F32)
             for n, h in enumerate(hs)]
        p = [probabilities(s[n], i) for n in range(heads)]
        for n, h in enumerate(hs):
            o = jnp.dot(p[n], h["v"][kv], preferred_element_type=F32)
            o_ref[0, rows, h["cols"]] = (o[:, :hd] / o[:, hd:hd + 1]).astype(o_ref.dtype)


def _moba(u, cos, sin, batch, seq, heads):
    hd = MOBA_HEAD_DIM
    nb = seq // MOBA_BLOCK
    per = MOBA_HEADS_PER_STEP
    groups = heads // per
    blk3 = lambda off: pl.BlockSpec((1, seq, per * hd), lambda b, h: (b, 0, h + off))
    tab = pl.BlockSpec((seq, hd), lambda b, h: (0, 0))
    return pl.pallas_call(
        functools.partial(_moba_body, nb=nb, heads=per),
        grid=(batch, groups),
        in_specs=[blk3(0), blk3(groups), blk3(2 * groups), tab, tab],
        out_specs=pl.BlockSpec((1, seq, per * hd), lambda b, h: (b, 0, h)),
        out_shape=jax.ShapeDtypeStruct((batch, seq, heads * hd), BF16),
        compiler_params=_params("parallel", "parallel"),
        name="moba_attention",
    )(u, u, u, cos, sin)


def _split3(x):
    hi = x.astype(BF16)
    r = x - hi.astype(F32)
    mid = r.astype(BF16)
    lo = (r - mid.astype(F32)).astype(BF16)
    return hi, mid, lo


def _gla_body(q_ref, k_ref, v_ref, gr_ref, ga_ref, up_ref, bias_ref, g_ref, o_ref,
              st_ref, la_ref, *, dk):
    c = GLA_CHUNK
    nbc = GLA_BLOCK_CHUNKS
    mid = nbc // 2
    blk = c * nbc
    scale = dk ** -0.5
    row = lax.broadcasted_iota(jnp.int32, (blk, blk), 0)
    col = lax.broadcasted_iota(jnp.int32, (blk, blk), 1)
    shift = c.bit_length() - 1
    row_c = jnp.right_shift(row, shift)
    col_c = jnp.right_shift(col, shift)
    below = row_c > col_c
    diag = (row_c == col_c) & (col <= row)
    tri = jnp.where(diag, 1.0, 0.0).astype(BF16)

    rank = GLA_GATE_RANK
    g = ga_ref[0]
    g_hi = g.astype(BF16).astype(F32)
    lane = lax.broadcasted_iota(jnp.int32, g.shape, 1)
    lhs = jnp.where((lane >= rank) & (lane < 2 * rank), g - g_hi, g_hi).astype(BF16)
    u = up_ref[...]
    u_hi = u.astype(BF16).astype(F32)
    rhs = jnp.where(lax.broadcasted_iota(jnp.int32, u.shape, 0) >= 2 * rank, u - u_hi,
                    u_hi).astype(BF16)
    z = jnp.dot(lhs, rhs, preferred_element_type=F32) + bias_ref[...]
    la_ref[...] = (jnp.minimum(z, 0.0) - jnp.log(1.0 + jnp.exp(-jnp.abs(z)))) / GLA_GATE_TAU
    st_ref[...] = jnp.zeros_like(st_ref)

    def stack(parts):
        return jnp.concatenate(parts, axis=0).astype(BF16)

    def block_step(bi, carry):
        rows = pl.ds(pl.multiple_of(bi * blk, blk), blk)
        hi, md, lo = _split3(la_ref[rows, :])
        b = (jnp.dot(tri, hi, preferred_element_type=F32)
             + jnp.dot(tri, md, preferred_element_type=F32)
             + jnp.dot(tri, lo, preferred_element_type=F32))
        q = q_ref[0, rows, :].astype(F32) * scale
        k = k_ref[0, rows, :].astype(F32)
        q_dec = q * jnp.exp(b)
        k_inv = k * jnp.exp(-b)
        tot = [b[(j + 1) * c - 1:(j + 1) * c, :] for j in range(nbc)]
        pre = [jnp.zeros_like(tot[0])]
        for j in range(nbc):
            pre.append(pre[-1] + tot[j])
        q_mid, k_mid, q_start, k_end = [], [], [], []
        for j in range(nbc):
            sl = slice(j * c, (j + 1) * c)
            q_mid.append(q_dec[sl] * jnp.exp(pre[j] - pre[mid]))
            k_mid.append(k_inv[sl] * jnp.exp(pre[mid] - pre[j]))
            q_start.append(q_dec[sl] * jnp.exp(pre[j]))
            k_end.append(k[sl] * jnp.exp(tot[j] - b[sl]) * jnp.exp(pre[nbc] - pre[j + 1]))
        a_diag = lax.dot_general(q_dec.astype(BF16), k_inv.astype(BF16), _NT,
                                 preferred_element_type=F32)
        a_off = lax.dot_general(stack(q_mid), stack(k_mid), _NT, preferred_element_type=F32)
        attn = jnp.where(below, a_off, jnp.where(diag, a_diag, 0.0)).astype(BF16)

        v = v_ref[0, rows, :]
        st = st_ref[...]
        o = (jnp.dot(attn, v, preferred_element_type=F32)
             + lax.dot_general(stack(q_start), st.astype(BF16), _NT, preferred_element_type=F32))
        st_ref[...] = st * jnp.exp(pre[nbc]) + lax.dot_general(v, stack(k_end), _TN,
                                                               preferred_element_type=F32)
        gr = gr_ref[0, rows, :].astype(F32)
        y = _rms(o, g_ref[...]) * (gr * jax.nn.sigmoid(gr))
        o_ref[0, rows, :] = y.astype(o_ref.dtype)
        return carry

    lax.fori_loop(0, q_ref.shape[1] // blk, block_step, 0, unroll=2)


def _gla(u, ga, up, bias, g, batch, seq, heads, dk, dv, q_off, k_off, v_off, r_off):
    qk = lambda off: pl.BlockSpec((1, seq, dk), lambda b, h: (b, 0, h + off))
    vv = lambda off: pl.BlockSpec((1, seq, dv), lambda b, h: (b, 0, h + off))
    return pl.pallas_call(
        functools.partial(_gla_body, dk=dk),
        grid=(batch, heads),
        in_specs=[qk(q_off), qk(k_off), vv(v_off), vv(r_off),
                  pl.BlockSpec((1, seq, LANES), lambda b, h: (b, 0, 0)),
                  pl.BlockSpec((LANES, dk), lambda b, h: (0, h)),
                  pl.BlockSpec((1, dk), lambda b, h: (0, h)),
                  pl.BlockSpec((1, dv), lambda b, h: (0, 0))],
        out_specs=pl.BlockSpec((1, seq, dv), lambda b, h: (b, 0, h)),
        out_shape=jax.ShapeDtypeStruct((batch, seq, heads * dv), BF16),
        scratch_shapes=[pltpu.VMEM((dv, dk), F32),
                        pltpu.VMEM((seq, dk), F32)],
        compiler_params=_params("parallel", "parallel"),
        name="gla",
    )(u, u, u, u, ga, up, bias, g)


def _tile(n, pref):
    return pref if n % pref == 0 else n


def _layer(x, pre_mix_g, w_in, gate_up, gate_bias, gla_norm_g, w_moba, w_gla, w_out,
           post_mix_g, pre_ffn_g, w_fg, w_fu, w_fd, post_ffn_g):
    batch, seq, d = x.shape
    t = batch * seq
    half = d // 2
    moba_heads = half // MOBA_HEAD_DIM
    gla_heads = d // 512
    dk = half // gla_heads
    dv = d // gla_heads
    n_main = 9 * half
    n_gate0 = n_main + GLA_GATE_RANK
    d_ff = w_fg.shape[1]

    tm = _tile(t, 1024)
    tn = _tile(d, 1024)
    tr = _tile(t, 256)
    x2 = x.reshape(t, d)
    vec = lambda g: g.reshape(1, -1)

    w_in_t = w_in.T
    w_ga = jnp.pad(jnp.tile(w_in_t[n_main:n_gate0], (3, 1)),
                   ((0, LANES - 3 * GLA_GATE_RANK), (0, 0))).T
    up_pad = jnp.pad(jnp.tile(gate_up, (3, 1)), ((0, LANES - 3 * GLA_GATE_RANK), (0, 0)))

    hd2 = MOBA_HEAD_DIM // 2
    inv_freq = ROPE_THETA ** (-jnp.arange(hd2, dtype=F32) / hd2)
    ang = jnp.arange(seq, dtype=F32)[:, None] * inv_freq[None, :]
    cos = jnp.concatenate([jnp.cos(ang), jnp.cos(ang)], axis=1)
    sin = jnp.concatenate([-jnp.sin(ang), jnp.sin(ang)], axis=1)

    h = _rmsnorm(x2, vec(pre_mix_g), tr)
    u = _matmul(h, w_in_t, n_main, tm, 1024 if n_main % 1024 == 0 else 512, BF16, "in_proj_main",
                transposed=True)
    gates = _matmul(h, w_in_t, 2 * d, tm, tn, BF16, "in_proj_gates", col0=n_gate0, transposed=True)
    ga = _matmul_small(h, w_ga, tm, F32, "in_proj_forget")

    u3 = u.reshape(batch, seq, n_main)
    o_moba = _moba(u3, cos, sin, batch, seq, moba_heads)
    o_gla = _gla(u3, ga.reshape(batch, seq, LANES), up_pad, vec(gate_bias), vec(gla_norm_g),
                 batch, seq, gla_heads, dk, dv,
                 q_off=3 * half // dk, k_off=4 * half // dk, v_off=5 * half // dv,
                 r_off=7 * half // dv)

    merged = _merge(o_moba.reshape(t, half), o_gla.reshape(t, d), w_moba, w_gla, gates, tm, 512)
    y = _matmul(merged, w_out, d, tm, tn, BF16, "out_proj")
    x1, h2 = _mid_norm(y, x2, vec(post_mix_g), vec(pre_ffn_g), tr)

    a, w_fd_bf = _ffn_up(h2, w_fg, w_fu, w_fd, tm, 512)
    tk = d_ff // 2 if (d_ff // 2) % LANES == 0 else d_ff
    y2 = _matmul_ksplit(a, w_fd_bf, tm, 512, tk, BF16, "ffn_down")
    out = _final_norm(y2, x1, vec(post_ffn_g), tr)
    return out.reshape(batch, seq, d)


def kernel(x, pre_mix_norm_g, w_in, gla_gate_up, gla_gate_bias, gla_out_norm_g, w_branch_moba,
           w_branch_gla, w_out, post_mix_norm_g, pre_ffn_norm_g, w_ffn_gate, w_ffn_up, w_ffn_down,
           post_ffn_norm_g):
    for layer in range(w_in.shape[0]):
        x = _layer(x, pre_mix_norm_g[layer], w_in[layer], gla_gate_up[layer], gla_gate_bias[layer],
                   gla_out_norm_g[layer], w_branch_moba[layer], w_branch_gla[layer], w_out[layer],
                   post_mix_norm_g[layer], pre_ffn_norm_g[layer], w_ffn_gate[layer],
                   w_ffn_up[layer], w_ffn_down[layer], post_ffn_norm_g[layer])
    return x
```

```python
import functools

import jax
import jax.numpy as jnp
from jax import lax
from jax.experimental import pallas as pl
from jax.experimental.pallas import tpu as pltpu

F32 = jnp.float32
BF16 = jnp.bfloat16

EPS = 1e-6
MOBA_HEAD_DIM = 128
MOBA_BLOCK = 256
MOBA_TOPK = 3
ROPE_THETA = 10000.0
LOG2_E = 1.4426950408889634
MASKED = -1e30
GLA_GATE_RANK = 16
GLA_GATE_TAU = 16.0
GLA_CHUNK = 64
GLA_BLOCK_CHUNKS = 4
LANES = 128
VMEM_LIMIT = 60 * 1024 * 1024

_NT = (((1,), (1,)), ((), ()))
_TN = (((0,), (0,)), ((), ()))


def _params(*sem):
    return pltpu.CompilerParams(dimension_semantics=sem, vmem_limit_bytes=VMEM_LIMIT)


def _rms(x, g):
    ms = jnp.mean(x * x, axis=-1, keepdims=True)
    return x * lax.rsqrt(ms + EPS) * g


def _rmsnorm_body(x_ref, g_ref, o_ref):
    o_ref[...] = _rms(x_ref[...], g_ref[...]).astype(o_ref.dtype)


def _rmsnorm(x, g, tm):
    m, d = x.shape
    return pl.pallas_call(
        _rmsnorm_body,
        grid=(m // tm,),
        in_specs=[pl.BlockSpec((tm, d), lambda i: (i, 0)),
                  pl.BlockSpec((1, d), lambda i: (0, 0))],
        out_specs=pl.BlockSpec((tm, d), lambda i: (i, 0)),
        out_shape=jax.ShapeDtypeStruct((m, d), BF16),
        compiler_params=_params("parallel"),
        name="rmsnorm",
    )(x, g)


def _mid_norm_body(y_ref, x_ref, g1_ref, g2_ref, x1_ref, h_ref):
    x1 = x_ref[...] + _rms(y_ref[...].astype(F32), g1_ref[...])
    x1_ref[...] = x1
    h_ref[...] = _rms(x1, g2_ref[...]).astype(h_ref.dtype)


def _mid_norm(y, x, g1, g2, tm):
    m, d = x.shape
    row = pl.BlockSpec((tm, d), lambda i: (i, 0))
    vec = pl.BlockSpec((1, d), lambda i: (0, 0))
    return pl.pallas_call(
        _mid_norm_body,
        grid=(m // tm,),
        in_specs=[row, row, vec, vec],
        out_specs=[row, row],
        out_shape=[jax.ShapeDtypeStruct((m, d), F32), jax.ShapeDtypeStruct((m, d), BF16)],
        compiler_params=_params("parallel"),
        name="mid_norm",
    )(y, x, g1, g2)


def _final_norm_body(y_ref, x_ref, g_ref, o_ref):
    o_ref[...] = x_ref[...] + _rms(y_ref[...].astype(F32), g_ref[...])


def _final_norm(y, x, g, tm):
    m, d = x.shape
    row = pl.BlockSpec((tm, d), lambda i: (i, 0))
    return pl.pallas_call(
        _final_norm_body,
        grid=(m // tm,),
        in_specs=[row, row, pl.BlockSpec((1, d), lambda i: (0, 0))],
        out_specs=row,
        out_shape=jax.ShapeDtypeStruct((m, d), F32),
        compiler_params=_params("parallel"),
        name="final_norm",
    )(y, x, g)


def _tile_copy(w_hbm, stage, sem, jj, *, tn, n, col0, transposed, ragged):
    width = n % tn if ragged else tn
    first = col0 + jj * tn
    if transposed:
        return pltpu.make_async_copy(w_hbm.at[pl.ds(first, width), :],
                                     stage.at[pl.ds(0, width), :], sem)
    return pltpu.make_async_copy(w_hbm.at[:, pl.ds(first, width)],
                                 stage.at[:, pl.ds(0, width)], sem)


def _stage_weight(w_hbm, stage, wbf, sem, *, tn, n, col0=0, transposed=False):
    j = pl.program_id(0)
    nj = pl.num_programs(0)
    copy = functools.partial(_tile_copy, w_hbm, stage, sem, tn=tn, n=n, col0=col0,
                             transposed=transposed)

    def for_tile(jj, act):
        if n % tn:
            pl.when(jj < nj - 1)(lambda: act(copy(jj, ragged=False)))
            pl.when(jj == nj - 1)(lambda: act(copy(jj, ragged=True)))
        else:
            act(copy(jj, ragged=False))

    @pl.when(pl.program_id(1) == 0)
    def _():
        pl.when(j == 0)(lambda: for_tile(j, lambda c: c.start()))
        for_tile(j, lambda c: c.wait())
        if transposed:
            slab = min(512, stage.shape[1])
            for c0 in range(0, stage.shape[1], slab):
                wbf[c0:c0 + slab, :] = stage[:, c0:c0 + slab].T.astype(BF16)
        else:
            wbf[...] = stage[...].astype(BF16)
        pl.when(j + 1 < nj)(lambda: for_tile(j + 1, lambda c: c.start()))


def _weight_scratch(k, tn, transposed=False):
    return [pltpu.VMEM((tn, k) if transposed else (k, tn), F32), pltpu.VMEM((k, tn), BF16)]


_HBM = pl.BlockSpec(memory_space=pl.ANY)


def _mm_body(x_ref, w_hbm, o_ref, stage, wbf, sem, **tiles):
    _stage_weight(w_hbm, stage, wbf, sem.at[0], **tiles)
    o_ref[...] = jnp.dot(x_ref[...], wbf[...], preferred_element_type=F32).astype(o_ref.dtype)


def _matmul(x, w, n, tm, tn, out_dtype, name, col0=0, transposed=False):
    m, k = x.shape
    return pl.pallas_call(
        functools.partial(_mm_body, tn=tn, n=n, col0=col0, transposed=transposed),
        grid=(n // tn, m // tm),
        in_specs=[pl.BlockSpec((tm, k), lambda j, i: (i, 0)), _HBM],
        out_specs=pl.BlockSpec((tm, tn), lambda j, i: (i, j)),
        out_shape=jax.ShapeDtypeStruct((m, n), out_dtype),
        scratch_shapes=_weight_scratch(k, tn, transposed) + [pltpu.SemaphoreType.DMA((1,))],
        compiler_params=_params("arbitrary", "arbitrary"),
        name=name,
    )(x, w)


def _mm_small_body(x_ref, w_ref, o_ref):
    o_ref[...] = jnp.dot(x_ref[...], w_ref[...].astype(BF16),
                         preferred_element_type=F32).astype(o_ref.dtype)


def _matmul_small(x, w, tm, out_dtype, name):
    m, k = x.shape
    n = w.shape[1]
    return pl.pallas_call(
        _mm_small_body,
        grid=(m // tm,),
        in_specs=[pl.BlockSpec((tm, k), lambda i: (i, 0)), pl.BlockSpec((k, n), lambda i: (0, 0))],
        out_specs=pl.BlockSpec((tm, n), lambda i: (i, 0)),
        out_shape=jax.ShapeDtypeStruct((m, n), out_dtype),
        compiler_params=_params("parallel"),
        name=name,
    )(x, w)


def _merge_body(om_ref, og_ref, wm_hbm, wg_hbm, gm_ref, gg_ref, o_ref,
                wm_stage, wm_bf, wg_stage, wg_bf, sem, **tiles):
    _stage_weight(wm_hbm, wm_stage, wm_bf, sem.at[0], **tiles)
    _stage_weight(wg_hbm, wg_stage, wg_bf, sem.at[1], **tiles)
    ym = jnp.dot(om_ref[...], wm_bf[...], preferred_element_type=F32)
    yg = jnp.dot(og_ref[...], wg_bf[...], preferred_element_type=F32)
    merged = (jax.nn.sigmoid(gm_ref[...].astype(F32)) * ym
              + jax.nn.sigmoid(gg_ref[...].astype(F32)) * yg)
    o_ref[...] = merged.astype(o_ref.dtype)


def _merge(o_moba, o_gla, w_moba, w_gla, gates, tm, tn):
    m, km = o_moba.shape
    kg = o_gla.shape[1]
    d = w_moba.shape[1]
    nj = d // tn
    return pl.pallas_call(
        functools.partial(_merge_body, tn=tn, n=d),
        grid=(nj, m // tm),
        in_specs=[pl.BlockSpec((tm, km), lambda j, i: (i, 0)),
                  pl.BlockSpec((tm, kg), lambda j, i: (i, 0)),
                  _HBM, _HBM,
                  pl.BlockSpec((tm, tn), lambda j, i: (i, j)),
                  pl.BlockSpec((tm, tn), lambda j, i: (i, j + nj))],
        out_specs=pl.BlockSpec((tm, tn), lambda j, i: (i, j)),
        out_shape=jax.ShapeDtypeStruct((m, d), BF16),
        scratch_shapes=(_weight_scratch(km, tn) + _weight_scratch(kg, tn)
                        + [pltpu.SemaphoreType.DMA((2,))]),
        compiler_params=_params("arbitrary", "arbitrary"),
        name="branch_merge",
    )(o_moba, o_gla, w_moba, w_gla, gates, gates)


def _ffn_up_body(h_ref, wg_hbm, wu_hbm, wd_ref, o_ref, wd_bf_ref,
                 wg_stage, wg_bf, wu_stage, wu_bf, sem, **tiles):
    _stage_weight(wg_hbm, wg_stage, wg_bf, sem.at[0], **tiles)
    _stage_weight(wu_hbm, wu_stage, wu_bf, sem.at[1], **tiles)
    h = h_ref[...]
    a = jnp.dot(h, wg_bf[...], preferred_element_type=F32)
    b = jnp.dot(h, wu_bf[...], preferred_element_type=F32)
    o_ref[...] = (a * jax.nn.sigmoid(a) * b).astype(o_ref.dtype)
    wd_bf_ref[...] = wd_ref[...].astype(BF16)


def _ffn_up(h, w_gate, w_up, w_down, tm, tn):
    m, k = h.shape
    n = w_gate.shape[1]
    ni = m // tm
    steps = pl.cdiv(n, tn) * ni
    slab = -(-pl.cdiv(w_down.shape[0], steps) // 16) * 16
    last = pl.cdiv(w_down.shape[0], slab) - 1
    wd_spec = pl.BlockSpec((slab, w_down.shape[1]), lambda j, i: (jnp.minimum(j * ni + i, last), 0))
    return pl.pallas_call(
        functools.partial(_ffn_up_body, tn=tn, n=n),
        grid=(pl.cdiv(n, tn), ni),
        in_specs=[pl.BlockSpec((tm, k), lambda j, i: (i, 0)), _HBM, _HBM, wd_spec],
        out_specs=[pl.BlockSpec((tm, tn), lambda j, i: (i, j)), wd_spec],
        out_shape=[jax.ShapeDtypeStruct((m, n), BF16), jax.ShapeDtypeStruct(w_down.shape, BF16)],
        scratch_shapes=(_weight_scratch(k, tn) + _weight_scratch(k, tn)
                        + [pltpu.SemaphoreType.DMA((2,))]),
        compiler_params=_params("arbitrary", "arbitrary"),
        name="ffn_up",
    )(h, w_gate, w_up, w_down)


def _mm_acc_body(x_ref, w_ref, o_ref, acc_ref):
    kk = pl.program_id(2)

    @pl.when(kk == 0)
    def _():
        acc_ref[...] = jnp.zeros_like(acc_ref)

    acc_ref[...] += jnp.dot(x_ref[...], w_ref[...], preferred_element_type=F32)

    @pl.when(kk == pl.num_programs(2) - 1)
    def _():
        o_ref[...] = acc_ref[...].astype(o_ref.dtype)


def _matmul_ksplit(x, w, tm, tn, tk, out_dtype, name):
    m, k = x.shape
    n = w.shape[1]
    return pl.pallas_call(
        _mm_acc_body,
        grid=(n // tn, m // tm, k // tk),
        in_specs=[pl.BlockSpec((tm, tk), lambda j, i, kk: (i, kk)),
                  pl.BlockSpec((tk, tn), lambda j, i, kk: (kk, j))],
        out_specs=pl.BlockSpec((tm, tn), lambda j, i, kk: (i, j)),
        out_shape=jax.ShapeDtypeStruct((m, n), out_dtype),
        scratch_shapes=[pltpu.VMEM((tm, tn), F32)],
        compiler_params=_params("parallel", "parallel", "arbitrary"),
        name=name,
    )(x, w)


def _moba_setup(q_ref, k_ref, v_ref, cos_ref, sin_ref, o_ref, *, nb, heads):
    blk = MOBA_BLOCK
    hd = MOBA_HEAD_DIM
    seq = q_ref.shape[1]
    scale = hd ** -0.5
    cos = cos_ref[...]
    sin = sin_ref[...]
    row = lax.broadcasted_iota(jnp.int32, (blk, blk), 0)
    col = lax.broadcasted_iota(jnp.int32, (blk, blk), 1)
    causal = col <= row

    r_l = lax.broadcasted_iota(jnp.int32, (LANES, LANES), 0)
    c_l = lax.broadcasted_iota(jnp.int32, (LANES, LANES), 1)
    sum_wins = jnp.where((c_l == r_l % nb) & (r_l < nb * nb), 1.0, 0.0).astype(BF16)
    key_lane = lax.broadcasted_iota(jnp.int32, (seq, LANES), 1)
    key_block = lax.broadcasted_iota(jnp.int32, (seq, LANES), 0) // blk
    block_ind = jnp.where(key_lane == key_block, 1.0, 0.0).astype(BF16)
    ones_col = jnp.where(key_lane == 0, 1.0, 0.0).astype(BF16)

    lane = lax.broadcasted_iota(jnp.int32, (blk, LANES), 1)
    lane_j = lane // nb
    lane_n = lane % nb
    lane_m = (lane_n - lane_j + nb) % nb
    tie = jnp.where(lane_m < lane_n, 1.0, 0.0)

    def rope(t):
        t = t.astype(F32)
        return t * cos + pltpu.roll(t, hd // 2, axis=1) * sin

    def prepare(h):
        cols = slice(h * hd, (h + 1) * hd)
        qr = rope(q_ref[0, :, cols])
        kr = rope(k_ref[0, :, cols])
        kmean = jnp.mean(kr.reshape(nb, blk, hd), axis=1)
        parts = [kmean] + [jnp.roll(kmean, j, axis=0) - kmean for j in range(1, nb)]
        parts.append(jnp.zeros((LANES - nb * nb, hd), F32))
        diff = jnp.concatenate(parts, axis=0)
        q_hi = qr.astype(BF16)
        q_lo = (qr - q_hi.astype(F32)).astype(BF16)
        d_hi = diff.astype(BF16)
        d_lo = (diff - d_hi.astype(F32)).astype(BF16)
        gate = lax.dot_general(jnp.concatenate([q_hi, q_lo, q_hi], axis=1),
                               jnp.concatenate([d_hi, d_hi, d_lo], axis=1), _NT,
                               preferred_element_type=F32)
        return dict(
            cols=cols, gate=gate,
            q=(qr * (scale * LOG2_E)).astype(BF16),
            k=jnp.concatenate([kr.astype(BF16), block_ind], axis=1),
            v=jnp.concatenate([v_ref[0, :, cols], ones_col], axis=1))

    def block_bias(gate, i):
        finite = jnp.abs(gate) < jnp.inf
        if i <= MOBA_TOPK:
            keep = finite
        else:
            compared = (lane_j >= 1) & (lane_j < nb) & (lane_n < i) & (lane_m < i)
            wins = jnp.where(compared, jnp.where(gate > 0.0, 1.0, jnp.where(gate == 0.0, tie, 0.0)), 0.0)
            wins = jnp.where(lane < nb, jnp.where(finite, 0.0, float(MOBA_TOPK)), wins)
            rank = jnp.dot(wins.astype(BF16), sum_wins, preferred_element_type=F32)
            keep = rank < MOBA_TOPK
        return jnp.where(lane < i, jnp.where(keep, 0.0, MASKED), jnp.where(lane == i, 0.0, MASKED))

    def probabilities(s, i):
        own = jnp.where(causal, s[:, i * blk:], -jnp.inf)
        s = jnp.concatenate([s[:, :i * blk], own], axis=1) if i else own
        m = jnp.max(s, axis=1, keepdims=True)
        return jnp.exp2(s - m).astype(BF16)

    hs = [prepare(h) for h in range(heads)]

    def step(i):
        rows = slice(i * blk, (i + 1) * blk)
        kv = slice(0, (i + 1) * blk)
        q_aug = [jnp.concatenate([h["q"][rows], block_bias(h["gate"][rows], i).astype(BF16)], axis=1)
                 for h in hs]
        s = [lax.dot_general(q_aug[n], h["k"][kv], _NT, preferred_element_type=F32)
             for n, h in enumerate(hs)]
        p = [probabilities(s[n], i) for n in range(heads)]
        for n, h in enumerate(hs):
            o = jnp.dot(p[n], h["v"][kv], preferred_element_type=F32)
            o_ref[0, rows, h["cols"]] = (o[:, :hd] / o[:, hd:hd + 1]).astype(o_ref.dtype)

    return step


def _split2(x):
    hi = x.astype(BF16)
    return hi, (x - hi.astype(F32)).astype(BF16)


def _gla_setup(q_ref, k_ref, v_ref, gr_ref, ga_ref, up_ref, bias_ref, g_ref, o_ref,
               st_ref, la_ref, *, dk):
    c = GLA_CHUNK
    nbc = GLA_BLOCK_CHUNKS
    mid = nbc // 2
    blk = c * nbc
    scale = dk ** -0.5
    row = lax.broadcasted_iota(jnp.int32, (blk, blk), 0)
    col = lax.broadcasted_iota(jnp.int32, (blk, blk), 1)
    shift = c.bit_length() - 1
    row_c = jnp.right_shift(row, shift)
    col_c = jnp.right_shift(col, shift)
    below = row_c > col_c
    diag = (row_c == col_c) & (col <= row)
    tri = jnp.where(diag, 1.0, 0.0).astype(BF16)

    rank = GLA_GATE_RANK
    g = ga_ref[0]
    g_hi = g.astype(BF16).astype(F32)
    lane = lax.broadcasted_iota(jnp.int32, g.shape, 1)
    lhs = jnp.where((lane >= rank) & (lane < 2 * rank), g - g_hi, g_hi).astype(BF16)
    u = up_ref[...]
    u_hi = u.astype(BF16).astype(F32)
    rhs = jnp.where(lax.broadcasted_iota(jnp.int32, u.shape, 0) >= 2 * rank, u - u_hi,
                    u_hi).astype(BF16)
    z = jnp.dot(lhs, rhs, preferred_element_type=F32) + bias_ref[...]
    la_ref[...] = (jnp.minimum(z, 0.0) - jnp.log(1.0 + jnp.exp(-jnp.abs(z)))) / GLA_GATE_TAU
    st_ref[...] = jnp.zeros_like(st_ref)

    def stack(parts):
        return jnp.concatenate(parts, axis=0).astype(BF16)

    def block_step(bi):
        rows = slice(bi * blk, (bi + 1) * blk)
        hi, lo = _split2(la_ref[rows, :])
        b = (jnp.dot(tri, hi, preferred_element_type=F32)
             + jnp.dot(tri, lo, preferred_element_type=F32))
        q = q_ref[0, rows, :].astype(F32) * scale
        k = k_ref[0, rows, :].astype(F32)
        q_dec = q * jnp.exp(b)
        k_inv = k * jnp.exp(-b)
        tot = [b[(j + 1) * c - 1:(j + 1) * c, :] for j in range(nbc)]
        pre = [jnp.zeros_like(tot[0])]
        for j in range(nbc):
            pre.append(pre[-1] + tot[j])
        q_mid, k_mid, q_start, k_end = [], [], [], []
        for j in range(nbc):
            sl = slice(j * c, (j + 1) * c)
            q_mid.append(q_dec[sl] * jnp.exp(pre[j] - pre[mid]) if j else q_dec[sl])
            k_mid.append(k_inv[sl] * jnp.exp(pre[mid] - pre[j]) if j < nbc - 1 else k_inv[sl])
            q_start.append(q_dec[sl] * jnp.exp(pre[j]))
            k_end.append(k[sl] * jnp.exp((pre[nbc] - pre[j]) - b[sl]))
        a_diag = lax.dot_general(q_dec.astype(BF16), k_inv.astype(BF16), _NT,
                                 preferred_element_type=F32)
        a_off = lax.dot_general(stack(q_mid), stack(k_mid), _NT, preferred_element_type=F32)
        attn = jnp.where(below, a_off, jnp.where(diag, a_diag, 0.0)).astype(BF16)

        v = v_ref[0, rows, :]
        st = st_ref[...]
        o = (jnp.dot(attn, v, preferred_element_type=F32)
             + lax.dot_general(stack(q_start), st.astype(BF16), _NT, preferred_element_type=F32))
        st_ref[...] = st * jnp.exp(pre[nbc]) + lax.dot_general(v, stack(k_end), _TN,
                                                               preferred_element_type=F32)
        gr = gr_ref[0, rows, :].astype(F32)
        y = _rms(o, g_ref[...]) * (gr * jax.nn.sigmoid(gr))
        o_ref[0, rows, :] = y.astype(o_ref.dtype)

    return block_step


def _mixers_body(mq_ref, mk_ref, mv_ref, cos_ref, sin_ref, gq_ref, gk_ref, gv_ref, gr_ref, ga_ref,
                 up_ref, bias_ref, g_ref, om_ref, og_ref, st_ref, la_ref, *, nb, moba_heads, dk):
    moba_step = _moba_setup(mq_ref, mk_ref, mv_ref, cos_ref, sin_ref, om_ref, nb=nb, heads=moba_heads)
    gla_step = _gla_setup(gq_ref, gk_ref, gv_ref, gr_ref, ga_ref, up_ref, bias_ref, g_ref, og_ref,
                          st_ref, la_ref, dk=dk)
    for i in range(nb):
        gla_step(i)
        moba_step(i)


def _mixers(u, cos, sin, ga, up, bias, g, batch, seq, moba_heads, gla_heads, dk, dv,
            q_off, k_off, v_off, r_off):
    hd = MOBA_HEAD_DIM
    assert MOBA_BLOCK == GLA_CHUNK * GLA_BLOCK_CHUNKS and moba_heads % gla_heads == 0
    per = moba_heads // gla_heads
    nb = seq // MOBA_BLOCK
    mb = lambda off: pl.BlockSpec((1, seq, per * hd), lambda b, h: (b, 0, h + off))
    qk = lambda off: pl.BlockSpec((1, seq, dk), lambda b, h: (b, 0, h + off))
    vv = lambda off: pl.BlockSpec((1, seq, dv), lambda b, h: (b, 0, h + off))
    tab = pl.BlockSpec((seq, hd), lambda b, h: (0, 0))
    return pl.pallas_call(
        functools.partial(_mixers_body, nb=nb, moba_heads=per, dk=dk),
        grid=(batch, gla_heads),
        in_specs=[mb(0), mb(gla_heads), mb(2 * gla_heads), tab, tab,
                  qk(q_off), qk(k_off), vv(v_off), vv(r_off),
                  pl.BlockSpec((1, seq, LANES), lambda b, h: (b, 0, 0)),
                  pl.BlockSpec((LANES, dk), lambda b, h: (0, h)),
                  pl.BlockSpec((1, dk), lambda b, h: (0, h)),
                  pl.BlockSpec((1, dv), lambda b, h: (0, 0))],
        out_specs=[pl.BlockSpec((1, seq, per * hd), lambda b, h: (b, 0, h)),
                   pl.BlockSpec((1, seq, dv), lambda b, h: (b, 0, h))],
        out_shape=[jax.ShapeDtypeStruct((batch, seq, moba_heads * hd), BF16),
                   jax.ShapeDtypeStruct((batch, seq, gla_heads * dv), BF16)],
        scratch_shapes=[pltpu.VMEM((dv, dk), F32),
                        pltpu.VMEM((seq, dk), F32)],
        compiler_params=_params("parallel", "parallel"),
        name="token_mixers",
    )(u, u, u, cos, sin, u, u, u, u, ga, up, bias, g)


def _tile(n, pref):
    return pref if n % pref == 0 else n


def _layer(x, pre_mix_g, w_in, gate_up, gate_bias, gla_norm_g, w_moba, w_gla, w_out,
           post_mix_g, pre_ffn_g, w_fg, w_fu, w_fd, post_ffn_g):
    batch, seq, d = x.shape
    t = batch * seq
    half = d // 2
    moba_heads = half // MOBA_HEAD_DIM
    gla_heads = d // 512
    dk = half // gla_heads
    dv = d // gla_heads
    n_main = 9 * half
    n_gate0 = n_main + GLA_GATE_RANK
    d_ff = w_fg.shape[1]

    tm = _tile(t, 1024)
    tn = _tile(d, 1024)
    tr = _tile(t, 256)
    x2 = x.reshape(t, d)
    vec = lambda g: g.reshape(1, -1)

    w_in_t = w_in.T
    w_ga = jnp.pad(jnp.tile(w_in_t[n_main:n_gate0], (3, 1)),
                   ((0, LANES - 3 * GLA_GATE_RANK), (0, 0))).T
    up_pad = jnp.pad(jnp.tile(gate_up, (3, 1)), ((0, LANES - 3 * GLA_GATE_RANK), (0, 0)))

    hd2 = MOBA_HEAD_DIM // 2
    inv_freq = ROPE_THETA ** (-jnp.arange(hd2, dtype=F32) / hd2)
    ang = jnp.arange(seq, dtype=F32)[:, None] * inv_freq[None, :]
    cos = jnp.concatenate([jnp.cos(ang), jnp.cos(ang)], axis=1)
    sin = jnp.concatenate([-jnp.sin(ang), jnp.sin(ang)], axis=1)

    h = _rmsnorm(x2, vec(pre_mix_g), tr)
    u = _matmul(h, w_in_t, n_main, tm, 1024 if n_main % 1024 == 0 else 512, BF16, "in_proj_main",
                transposed=True)
    gates = _matmul(h, w_in_t, 2 * d, tm, tn, BF16, "in_proj_gates", col0=n_gate0, transposed=True)
    ga = _matmul_small(h, w_ga, tm, F32, "in_proj_forget")

    u3 = u.reshape(batch, seq, n_main)
    o_moba, o_gla = _mixers(u3, cos, sin, ga.reshape(batch, seq, LANES), up_pad, vec(gate_bias),
                            vec(gla_norm_g), batch, seq, moba_heads, gla_heads, dk, dv,
                            q_off=3 * half // dk, k_off=4 * half // dk, v_off=5 * half // dv,
                            r_off=7 * half // dv)

    merged = _merge(o_moba.reshape(t, half), o_gla.reshape(t, d), w_moba, w_gla, gates, tm, 512)
    y = _matmul(merged, w_out, d, tm, tn, BF16, "out_proj")
    x1, h2 = _mid_norm(y, x2, vec(post_mix_g), vec(pre_ffn_g), tr)

    a, w_fd_bf = _ffn_up(h2, w_fg, w_fu, w_fd, tm, 512)
    tk = d_ff // 2 if (d_ff // 2) % LANES == 0 else d_ff
    y2 = _matmul_ksplit(a, w_fd_bf, tm, 512, tk, BF16, "ffn_down")
    out = _final_norm(y2, x1, vec(post_ffn_g), tr)
    return out.reshape(batch, seq, d)


def kernel(x, pre_mix_norm_g, w_in, gla_gate_up, gla_gate_bias, gla_out_norm_g, w_branch_moba,
           w_branch_gla, w_out, post_mix_norm_g, pre_ffn_norm_g, w_ffn_gate, w_ffn_up, w_ffn_down,
           post_ffn_norm_g):
    for layer in range(w_in.shape[0]):
        x = _layer(x, pre_mix_norm_g[layer], w_in[layer], gla_gate_up[layer], gla_gate_bias[layer],
                   gla_out_norm_g[layer], w_branch_moba[layer], w_branch_gla[layer], w_out[layer],
                   post_mix_norm_g[layer], pre_ffn_norm_g[layer], w_ffn_gate[layer],
                   w_ffn_up[layer], w_ffn_down[layer], post_ffn_norm_g[layer])
    return x
```

```python
import functools

import jax
import jax.numpy as jnp
from jax import lax
from jax.experimental import pallas as pl
from jax.experimental.pallas import tpu as pltpu

F32 = jnp.float32
BF16 = jnp.bfloat16

EPS = 1e-6
MOBA_HEAD_DIM = 128
MOBA_BLOCK = 256
MOBA_TOPK = 3
ROPE_THETA = 10000.0
LOG2_E = 1.4426950408889634
MASKED = -1e30
GLA_GATE_RANK = 16
GLA_GATE_TAU = 16.0
GLA_CHUNK = 64
GLA_BLOCK_CHUNKS = 4
LANES = 128
VMEM_LIMIT = 60 * 1024 * 1024

_NT = (((1,), (1,)), ((), ()))
_TN = (((0,), (0,)), ((), ()))


def _params(*sem):
    return pltpu.CompilerParams(dimension_semantics=sem, vmem_limit_bytes=VMEM_LIMIT)


def _rms(x, g):
    ms = jnp.mean(x * x, axis=-1, keepdims=True)
    return x * lax.rsqrt(ms + EPS) * g


def _rmsnorm_body(x_ref, g_ref, o_ref):
    o_ref[...] = _rms(x_ref[...], g_ref[...]).astype(o_ref.dtype)


def _rmsnorm(x, g, tm):
    m, d = x.shape
    return pl.pallas_call(
        _rmsnorm_body,
        grid=(m // tm,),
        in_specs=[pl.BlockSpec((tm, d), lambda i: (i, 0)),
                  pl.BlockSpec((1, d), lambda i: (0, 0))],
        out_specs=pl.BlockSpec((tm, d), lambda i: (i, 0)),
        out_shape=jax.ShapeDtypeStruct((m, d), BF16),
        compiler_params=_params("parallel"),
        name="rmsnorm",
    )(x, g)


def _mid_norm_body(y_ref, x_ref, g1_ref, g2_ref, x1_ref, h_ref):
    x1 = x_ref[...] + _rms(y_ref[...].astype(F32), g1_ref[...])
    x1_ref[...] = x1
    h_ref[...] = _rms(x1, g2_ref[...]).astype(h_ref.dtype)


def _mid_norm(y, x, g1, g2, tm):
    m, d = x.shape
    row = pl.BlockSpec((tm, d), lambda i: (i, 0))
    vec = pl.BlockSpec((1, d), lambda i: (0, 0))
    return pl.pallas_call(
        _mid_norm_body,
        grid=(m // tm,),
        in_specs=[row, row, vec, vec],
        out_specs=[row, row],
        out_shape=[jax.ShapeDtypeStruct((m, d), F32), jax.ShapeDtypeStruct((m, d), BF16)],
        compiler_params=_params("parallel"),
        name="mid_norm",
    )(y, x, g1, g2)


def _final_norm_body(y_ref, x_ref, g_ref, o_ref):
    o_ref[...] = x_ref[...] + _rms(y_ref[...].astype(F32), g_ref[...])


def _final_norm(y, x, g, tm):
    m, d = x.shape
    row = pl.BlockSpec((tm, d), lambda i: (i, 0))
    return pl.pallas_call(
        _final_norm_body,
        grid=(m // tm,),
        in_specs=[row, row, pl.BlockSpec((1, d), lambda i: (0, 0))],
        out_specs=row,
        out_shape=jax.ShapeDtypeStruct((m, d), F32),
        compiler_params=_params("parallel"),
        name="final_norm",
    )(y, x, g)


def _tile_copy(w_hbm, stage, sem, jj, *, tn, n, col0, transposed, ragged):
    width = n % tn if ragged else tn
    first = col0 + jj * tn
    if transposed:
        return pltpu.make_async_copy(w_hbm.at[pl.ds(first, width), :],
                                     stage.at[pl.ds(0, width), :], sem)
    return pltpu.make_async_copy(w_hbm.at[:, pl.ds(first, width)],
                                 stage.at[:, pl.ds(0, width)], sem)


def _stage_weight(w_hbm, stage, wbf, sem, *, tn, n, col0=0, transposed=False):
    j = pl.program_id(0)
    nj = pl.num_programs(0)
    copy = functools.partial(_tile_copy, w_hbm, stage, sem, tn=tn, n=n, col0=col0,
                             transposed=transposed)

    def for_tile(jj, act):
        if n % tn:
            pl.when(jj < nj - 1)(lambda: act(copy(jj, ragged=False)))
            pl.when(jj == nj - 1)(lambda: act(copy(jj, ragged=True)))
        else:
            act(copy(jj, ragged=False))

    @pl.when(pl.program_id(1) == 0)
    def _():
        pl.when(j == 0)(lambda: for_tile(j, lambda c: c.start()))
        for_tile(j, lambda c: c.wait())
        wbf[...] = stage[...].astype(BF16)
        pl.when(j + 1 < nj)(lambda: for_tile(j + 1, lambda c: c.start()))


def _weight_scratch(k, tn, transposed=False):
    shape = (tn, k) if transposed else (k, tn)
    return [pltpu.VMEM(shape, F32), pltpu.VMEM(shape, BF16)]


_HBM = pl.BlockSpec(memory_space=pl.ANY)


def _mm_body(x_ref, w_hbm, o_ref, stage, wbf, sem, **tiles):
    _stage_weight(w_hbm, stage, wbf, sem.at[0], **tiles)
    dims = _NT if tiles["transposed"] else (((1,), (0,)), ((), ()))
    o_ref[...] = lax.dot_general(x_ref[...], wbf[...], dims,
                                 preferred_element_type=F32).astype(o_ref.dtype)


def _matmul(x, w, n, tm, tn, out_dtype, name, col0=0, transposed=False):
    m, k = x.shape
    return pl.pallas_call(
        functools.partial(_mm_body, tn=tn, n=n, col0=col0, transposed=transposed),
        grid=(n // tn, m // tm),
        in_specs=[pl.BlockSpec((tm, k), lambda j, i: (i, 0)), _HBM],
        out_specs=pl.BlockSpec((tm, tn), lambda j, i: (i, j)),
        out_shape=jax.ShapeDtypeStruct((m, n), out_dtype),
        scratch_shapes=_weight_scratch(k, tn, transposed) + [pltpu.SemaphoreType.DMA((1,))],
        compiler_params=_params("arbitrary", "arbitrary"),
        name=name,
    )(x, w)


def _mm_small_body(x_ref, w_ref, o_ref):
    o_ref[...] = jnp.dot(x_ref[...], w_ref[...].astype(BF16),
                         preferred_element_type=F32).astype(o_ref.dtype)


def _matmul_small(x, w, tm, out_dtype, name):
    m, k = x.shape
    n = w.shape[1]
    return pl.pallas_call(
        _mm_small_body,
        grid=(m // tm,),
        in_specs=[pl.BlockSpec((tm, k), lambda i: (i, 0)), pl.BlockSpec((k, n), lambda i: (0, 0))],
        out_specs=pl.BlockSpec((tm, n), lambda i: (i, 0)),
        out_shape=jax.ShapeDtypeStruct((m, n), out_dtype),
        compiler_params=_params("parallel"),
        name=name,
    )(x, w)


def _merge_body(om_ref, og_ref, wm_hbm, wg_hbm, gm_ref, gg_ref, o_ref,
                wm_stage, wm_bf, wg_stage, wg_bf, sem, **tiles):
    _stage_weight(wm_hbm, wm_stage, wm_bf, sem.at[0], **tiles)
    _stage_weight(wg_hbm, wg_stage, wg_bf, sem.at[1], **tiles)
    ym = jnp.dot(om_ref[...], wm_bf[...], preferred_element_type=F32)
    yg = jnp.dot(og_ref[...], wg_bf[...], preferred_element_type=F32)
    merged = (jax.nn.sigmoid(gm_ref[...].astype(F32)) * ym
              + jax.nn.sigmoid(gg_ref[...].astype(F32)) * yg)
    o_ref[...] = merged.astype(o_ref.dtype)


def _merge(o_moba, o_gla, w_moba, w_gla, gates, tm, tn):
    m, km = o_moba.shape
    kg = o_gla.shape[1]
    d = w_moba.shape[1]
    nj = d // tn
    return pl.pallas_call(
        functools.partial(_merge_body, tn=tn, n=d),
        grid=(nj, m // tm),
        in_specs=[pl.BlockSpec((tm, km), lambda j, i: (i, 0)),
                  pl.BlockSpec((tm, kg), lambda j, i: (i, 0)),
                  _HBM, _HBM,
                  pl.BlockSpec((tm, tn), lambda j, i: (i, j)),
                  pl.BlockSpec((tm, tn), lambda j, i: (i, j + nj))],
        out_specs=pl.BlockSpec((tm, tn), lambda j, i: (i, j)),
        out_shape=jax.ShapeDtypeStruct((m, d), BF16),
        scratch_shapes=(_weight_scratch(km, tn) + _weight_scratch(kg, tn)
                        + [pltpu.SemaphoreType.DMA((2,))]),
        compiler_params=_params("arbitrary", "arbitrary"),
        name="branch_merge",
    )(o_moba, o_gla, w_moba, w_gla, gates, gates)


def _ffn_up_body(h_ref, wg_hbm, wu_hbm, wd_ref, o_ref, wd_bf_ref,
                 wg_stage, wg_bf, wu_stage, wu_bf, sem, **tiles):
    _stage_weight(wg_hbm, wg_stage, wg_bf, sem.at[0], **tiles)
    _stage_weight(wu_hbm, wu_stage, wu_bf, sem.at[1], **tiles)

    def compute(width):
        h = h_ref[...]
        a = jnp.dot(h, wg_bf[:, :width], preferred_element_type=F32)
        b = jnp.dot(h, wu_bf[:, :width], preferred_element_type=F32)
        o_ref[:, :width] = (a * jax.nn.sigmoid(a) * b).astype(o_ref.dtype)

    tn, short = tiles["tn"], tiles["n"] % tiles["tn"]
    if short:
        last = pl.num_programs(0) - 1
        pl.when(pl.program_id(0) < last)(lambda: compute(tn))
        pl.when(pl.program_id(0) == last)(lambda: compute(short))
    else:
        compute(tn)
    wd_bf_ref[...] = wd_ref[...].astype(BF16)


def _ffn_up(h, w_gate, w_up, w_down, tm, tn):
    m, k = h.shape
    n = w_gate.shape[1]
    ni = m // tm
    steps = pl.cdiv(n, tn) * ni
    slab = -(-pl.cdiv(w_down.shape[0], steps) // 16) * 16
    last = pl.cdiv(w_down.shape[0], slab) - 1
    wd_spec = pl.BlockSpec((slab, w_down.shape[1]), lambda j, i: (jnp.minimum(j * ni + i, last), 0))
    return pl.pallas_call(
        functools.partial(_ffn_up_body, tn=tn, n=n),
        grid=(pl.cdiv(n, tn), ni),
        in_specs=[pl.BlockSpec((tm, k), lambda j, i: (i, 0)), _HBM, _HBM, wd_spec],
        out_specs=[pl.BlockSpec((tm, tn), lambda j, i: (i, j)), wd_spec],
        out_shape=[jax.ShapeDtypeStruct((m, n), BF16), jax.ShapeDtypeStruct(w_down.shape, BF16)],
        scratch_shapes=(_weight_scratch(k, tn) + _weight_scratch(k, tn)
                        + [pltpu.SemaphoreType.DMA((2,))]),
        compiler_params=_params("arbitrary", "arbitrary"),
        name="ffn_up",
    )(h, w_gate, w_up, w_down)


def _mm_bf16_body(x_ref, w_ref, o_ref):
    o_ref[...] = jnp.dot(x_ref[...], w_ref[...], preferred_element_type=F32).astype(o_ref.dtype)


def _matmul_bf16(x, w, tm, tn, out_dtype, name):
    m, k = x.shape
    n = w.shape[1]
    return pl.pallas_call(
        _mm_bf16_body,
        grid=(n // tn, m // tm),
        in_specs=[pl.BlockSpec((tm, k), lambda j, i: (i, 0)),
                  pl.BlockSpec((k, tn), lambda j, i: (0, j))],
        out_specs=pl.BlockSpec((tm, tn), lambda j, i: (i, j)),
        out_shape=jax.ShapeDtypeStruct((m, n), out_dtype),
        compiler_params=_params("parallel", "parallel"),
        name=name,
    )(x, w)


def _moba_setup(q_ref, k_ref, v_ref, cos_ref, sin_ref, o_ref, *, nb, heads):
    blk = MOBA_BLOCK
    hd = MOBA_HEAD_DIM
    seq = q_ref.shape[1]
    scale = hd ** -0.5
    cos = cos_ref[...]
    sin = sin_ref[...]
    row = lax.broadcasted_iota(jnp.int32, (blk, blk), 0)
    col = lax.broadcasted_iota(jnp.int32, (blk, blk), 1)
    causal = col <= row

    r_l = lax.broadcasted_iota(jnp.int32, (LANES, LANES), 0)
    c_l = lax.broadcasted_iota(jnp.int32, (LANES, LANES), 1)
    sum_wins = jnp.where((c_l == r_l % nb) & (r_l < nb * nb), 1.0, 0.0).astype(BF16)
    key_lane = lax.broadcasted_iota(jnp.int32, (seq, LANES), 1)
    key_block = lax.broadcasted_iota(jnp.int32, (seq, LANES), 0) // blk
    block_ind = jnp.where(key_lane == key_block, 1.0, 0.0).astype(BF16)
    ones_col = jnp.where(key_lane == 0, 1.0, 0.0).astype(BF16)

    lane = lax.broadcasted_iota(jnp.int32, (blk, LANES), 1)
    lane_j = lane // nb
    lane_n = lane % nb
    lane_m = (lane_n - lane_j + nb) % nb
    tie = jnp.where(lane_m < lane_n, 1.0, 0.0)

    def rope(t):
        t = t.astype(F32)
        return t * cos + pltpu.roll(t, hd // 2, axis=1) * sin

    def prepare(h):
        cols = slice(h * hd, (h + 1) * hd)
        qr = rope(q_ref[0, :, cols])
        kr = rope(k_ref[0, :, cols])
        kmean = jnp.mean(kr.reshape(nb, blk, hd), axis=1)
        parts = [kmean] + [jnp.roll(kmean, j, axis=0) - kmean for j in range(1, nb)]
        parts.append(jnp.zeros((LANES - nb * nb, hd), F32))
        diff = jnp.concatenate(parts, axis=0)
        q_hi = qr.astype(BF16)
        q_lo = (qr - q_hi.astype(F32)).astype(BF16)
        d_hi = diff.astype(BF16)
        d_lo = (diff - d_hi.astype(F32)).astype(BF16)
        gate = lax.dot_general(jnp.concatenate([q_hi, q_lo, q_hi], axis=1),
                               jnp.concatenate([d_hi, d_hi, d_lo], axis=1), _NT,
                               preferred_element_type=F32)
        return dict(
            cols=cols, gate=gate,
            q=(qr * (scale * LOG2_E)).astype(BF16),
            k=jnp.concatenate([kr.astype(BF16), block_ind], axis=1),
            v=jnp.concatenate([v_ref[0, :, cols], ones_col], axis=1))

    def block_bias(gate, i):
        finite = jnp.abs(gate) < jnp.inf
        if i <= MOBA_TOPK:
            keep = finite
        else:
            compared = (lane_j >= 1) & (lane_j < nb) & (lane_n < i) & (lane_m < i)
            wins = jnp.where(compared, jnp.where(gate > 0.0, 1.0, jnp.where(gate == 0.0, tie, 0.0)), 0.0)
            wins = jnp.where(lane < nb, jnp.where(finite, 0.0, float(MOBA_TOPK)), wins)
            rank = jnp.dot(wins.astype(BF16), sum_wins, preferred_element_type=F32)
            keep = rank < MOBA_TOPK
        return jnp.where(lane < i, jnp.where(keep, 0.0, MASKED), jnp.where(lane == i, 0.0, MASKED))

    def probabilities(s, i):
        own = jnp.where(causal, s[:, i * blk:], -jnp.inf)
        s = jnp.concatenate([s[:, :i * blk], own], axis=1) if i else own
        m = jnp.max(s, axis=1, keepdims=True)
        return jnp.exp2(s - m).astype(BF16)

    hs = [prepare(h) for h in range(heads)]

    def step(i):
        rows = slice(i * blk, (i + 1) * blk)
        kv = slice(0, (i + 1) * blk)
        q_aug = [jnp.concatenate([h["q"][rows], block_bias(h["gate"][rows], i).astype(BF16)], axis=1)
                 for h in hs]
        s = [lax.dot_general(q_aug[n], h["k"][kv], _NT, preferred_element_type=F32)
             for n, h in enumerate(hs)]
        p = [probabilities(s[n], i) for n in range(heads)]
        for n, h in enumerate(hs):
            o = jnp.dot(p[n], h["v"][kv], preferred_element_type=F32)
            o_ref[0, rows, h["cols"]] = (o[:, :hd] / o[:, hd:hd + 1]).astype(o_ref.dtype)

    return step


def _split2(x):
    hi = x.astype(BF16)
    return hi, (x - hi.astype(F32)).astype(BF16)


def _gla_setup(q_ref, k_ref, v_ref, gr_ref, ga_ref, up_ref, bias_ref, g_ref, o_ref,
               st_ref, la_ref, *, dk):
    c = GLA_CHUNK
    nbc = GLA_BLOCK_CHUNKS
    mid = nbc // 2
    blk = c * nbc
    scale = dk ** -0.5
    row = lax.broadcasted_iota(jnp.int32, (blk, blk), 0)
    col = lax.broadcasted_iota(jnp.int32, (blk, blk), 1)
    shift = c.bit_length() - 1
    row_c = jnp.right_shift(row, shift)
    col_c = jnp.right_shift(col, shift)
    below = row_c > col_c
    diag = (row_c == col_c) & (col <= row)
    tri = jnp.where(diag, 1.0, 0.0).astype(BF16)

    rank = GLA_GATE_RANK
    g = ga_ref[0]
    g_hi = g.astype(BF16).astype(F32)
    lane = lax.broadcasted_iota(jnp.int32, g.shape, 1)
    lhs = jnp.where((lane >= rank) & (lane < 2 * rank), g - g_hi, g_hi).astype(BF16)
    u = up_ref[...]
    u_hi = u.astype(BF16).astype(F32)
    rhs = jnp.where(lax.broadcasted_iota(jnp.int32, u.shape, 0) >= 2 * rank, u - u_hi,
                    u_hi).astype(BF16)
    z = jnp.dot(lhs, rhs, preferred_element_type=F32) + bias_ref[...]
    la_ref[...] = (jnp.minimum(z, 0.0) - jnp.log(1.0 + jnp.exp(-jnp.abs(z)))) / GLA_GATE_TAU
    st_ref[...] = jnp.zeros_like(st_ref)

    def stack(parts):
        return jnp.concatenate(parts, axis=0).astype(BF16)

    def block_step(bi):
        rows = slice(bi * blk, (bi + 1) * blk)
        hi, lo = _split2(la_ref[rows, :])
        b = (jnp.dot(tri, hi, preferred_element_type=F32)
             + jnp.dot(tri, lo, preferred_element_type=F32))
        q = q_ref[0, rows, :].astype(F32) * scale
        k = k_ref[0, rows, :].astype(F32)
        q_dec = q * jnp.exp(b)
        k_inv = k * jnp.exp(-b)
        tot = [b[(j + 1) * c - 1:(j + 1) * c, :] for j in range(nbc)]
        pre = [jnp.zeros_like(tot[0])]
        for j in range(nbc):
            pre.append(pre[-1] + tot[j])
        q_mid, k_mid, q_start, k_end = [], [], [], []
        for j in range(nbc):
            sl = slice(j * c, (j + 1) * c)
            q_mid.append(q_dec[sl] * jnp.exp(pre[j] - pre[mid]) if j else q_dec[sl])
            k_mid.append(k_inv[sl] * jnp.exp(pre[mid] - pre[j]) if j < nbc - 1 else k_inv[sl])
            q_start.append(q_dec[sl] * jnp.exp(pre[j]))
            k_end.append(k[sl] * jnp.exp((pre[nbc] - pre[j]) - b[sl]))
        a_diag = lax.dot_general(q_dec.astype(BF16), k_inv.astype(BF16), _NT,
                                 preferred_element_type=F32)
        a_off = lax.dot_general(stack(q_mid), stack(k_mid), _NT, preferred_element_type=F32)
        attn = jnp.where(below, a_off, jnp.where(diag, a_diag, 0.0)).astype(BF16)

        v = v_ref[0, rows, :]
        st = st_ref[...]
        o = (jnp.dot(attn, v, preferred_element_type=F32)
             + lax.dot_general(stack(q_start), st.astype(BF16), _NT, preferred_element_type=F32))
        st_ref[...] = st * jnp.exp(pre[nbc]) + lax.dot_general(v, stack(k_end), _TN,
                                                               preferred_element_type=F32)
        gr = gr_ref[0, rows, :].astype(F32)
        y = _rms(o, g_ref[...]) * (gr * jax.nn.sigmoid(gr))
        o_ref[0, rows, :] = y.astype(o_ref.dtype)

    return block_step


def _mixers_body(mq_ref, mk_ref, mv_ref, cos_ref, sin_ref, gq_ref, gk_ref, gv_ref, gr_ref, ga_ref,
                 up_ref, bias_ref, g_ref, om_ref, og_ref, st_ref, la_ref, *, nb, moba_heads, dk):
    moba_step = _moba_setup(mq_ref, mk_ref, mv_ref, cos_ref, sin_ref, om_ref, nb=nb, heads=moba_heads)
    gla_step = _gla_setup(gq_ref, gk_ref, gv_ref, gr_ref, ga_ref, up_ref, bias_ref, g_ref, og_ref,
                          st_ref, la_ref, dk=dk)
    for i in range(nb):
        gla_step(i)
        moba_step(i)


def _mixers(u, cos, sin, ga, up, bias, g, batch, seq, moba_heads, gla_heads, dk, dv,
            q_off, k_off, v_off, r_off):
    hd = MOBA_HEAD_DIM
    assert MOBA_BLOCK == GLA_CHUNK * GLA_BLOCK_CHUNKS and moba_heads % gla_heads == 0
    per = moba_heads // gla_heads
    nb = seq // MOBA_BLOCK
    mb = lambda off: pl.BlockSpec((1, seq, per * hd), lambda b, h: (b, 0, h + off))
    qk = lambda off: pl.BlockSpec((1, seq, dk), lambda b, h: (b, 0, h + off))
    vv = lambda off: pl.BlockSpec((1, seq, dv), lambda b, h: (b, 0, h + off))
    tab = pl.BlockSpec((seq, hd), lambda b, h: (0, 0))
    return pl.pallas_call(
        functools.partial(_mixers_body, nb=nb, moba_heads=per, dk=dk),
        grid=(batch, gla_heads),
        in_specs=[mb(0), mb(gla_heads), mb(2 * gla_heads), tab, tab,
                  qk(q_off), qk(k_off), vv(v_off), vv(r_off),
                  pl.BlockSpec((1, seq, LANES), lambda b, h: (b, 0, 0)),
                  pl.BlockSpec((LANES, dk), lambda b, h: (0, h)),
                  pl.BlockSpec((1, dk), lambda b, h: (0, h)),
                  pl.BlockSpec((1, dv), lambda b, h: (0, 0))],
        out_specs=[pl.BlockSpec((1, seq, per * hd), lambda b, h: (b, 0, h)),
                   pl.BlockSpec((1, seq, dv), lambda b, h: (b, 0, h))],
        out_shape=[jax.ShapeDtypeStruct((batch, seq, moba_heads * hd), BF16),
                   jax.ShapeDtypeStruct((batch, seq, gla_heads * dv), BF16)],
        scratch_shapes=[pltpu.VMEM((dv, dk), F32),
                        pltpu.VMEM((seq, dk), F32)],
        compiler_params=_params("parallel", "parallel"),
        name="token_mixers",
    )(u, u, u, cos, sin, u, u, u, u, ga, up, bias, g)


def _tile(n, pref):
    return pref if n % pref == 0 else n


def _layer(x, pre_mix_g, w_in, gate_up, gate_bias, gla_norm_g, w_moba, w_gla, w_out,
           post_mix_g, pre_ffn_g, w_fg, w_fu, w_fd, post_ffn_g):
    batch, seq, d = x.shape
    t = batch * seq
    half = d // 2
    moba_heads = half // MOBA_HEAD_DIM
    gla_heads = d // 512
    dk = half // gla_heads
    dv = d // gla_heads
    n_main = 9 * half
    n_gate0 = n_main + GLA_GATE_RANK
    d_ff = w_fg.shape[1]

    tm = _tile(t, 1024)
    tn = _tile(d, 1024)
    tr = _tile(t, 256)
    x2 = x.reshape(t, d)
    vec = lambda g: g.reshape(1, -1)

    w_in_t = w_in.T
    w_ga = jnp.pad(jnp.tile(w_in_t[n_main:n_gate0], (3, 1)),
                   ((0, LANES - 3 * GLA_GATE_RANK), (0, 0))).T
    up_pad = jnp.pad(jnp.tile(gate_up, (3, 1)), ((0, LANES - 3 * GLA_GATE_RANK), (0, 0)))

    hd2 = MOBA_HEAD_DIM // 2
    inv_freq = ROPE_THETA ** (-jnp.arange(hd2, dtype=F32) / hd2)
    ang = jnp.arange(seq, dtype=F32)[:, None] * inv_freq[None, :]
    cos = jnp.concatenate([jnp.cos(ang), jnp.cos(ang)], axis=1)
    sin = jnp.concatenate([-jnp.sin(ang), jnp.sin(ang)], axis=1)

    h = _rmsnorm(x2, vec(pre_mix_g), tr)
    u = _matmul(h, w_in_t, n_main, tm, 1024 if n_main % 1024 == 0 else 512, BF16, "in_proj_main",
                transposed=True)
    gates = _matmul(h, w_in_t, 2 * d, tm, tn, BF16, "in_proj_gates", col0=n_gate0, transposed=True)
    ga = _matmul_small(h, w_ga, tm, F32, "in_proj_forget")

    u3 = u.reshape(batch, seq, n_main)
    o_moba, o_gla = _mixers(u3, cos, sin, ga.reshape(batch, seq, LANES), up_pad, vec(gate_bias),
                            vec(gla_norm_g), batch, seq, moba_heads, gla_heads, dk, dv,
                            q_off=3 * half // dk, k_off=4 * half // dk, v_off=5 * half // dv,
                            r_off=7 * half // dv)

    merged = _merge(o_moba.reshape(t, half), o_gla.reshape(t, d), w_moba, w_gla, gates, tm, 512)
    y = _matmul(merged, w_out, d, tm, tn, BF16, "out_proj")
    x1, h2 = _mid_norm(y, x2, vec(post_mix_g), vec(pre_ffn_g), tr)

    a, w_fd_bf = _ffn_up(h2, w_fg, w_fu, w_fd, tm, 512)
    y2 = _matmul_bf16(a, w_fd_bf, _tile(t, 512), 512, BF16, "ffn_down")
    out = _final_norm(y2, x1, vec(post_ffn_g), tr)
    return out.reshape(batch, seq, d)


def kernel(x, pre_mix_norm_g, w_in, gla_gate_up, gla_gate_bias, gla_out_norm_g, w_branch_moba,
           w_branch_gla, w_out, post_mix_norm_g, pre_ffn_norm_g, w_ffn_gate, w_ffn_up, w_ffn_down,
           post_ffn_norm_g):
    for layer in range(w_in.shape[0]):
        x = _layer(x, pre_mix_norm_g[layer], w_in[layer], gla_gate_up[layer], gla_gate_bias[layer],
                   gla_out_norm_g[layer], w_branch_moba[layer], w_branch_gla[layer], w_out[layer],
                   post_mix_norm_g[layer], pre_ffn_norm_g[layer], w_ffn_gate[layer],
                   w_ffn_up[layer], w_ffn_down[layer], post_ffn_norm_g[layer])
    return x
```

```python
import functools

import jax
import jax.numpy as jnp
from jax import lax
from jax.experimental import pallas as pl
from jax.experimental.pallas import tpu as pltpu

F32 = jnp.float32
BF16 = jnp.bfloat16

EPS = 1e-6
MOBA_HEAD_DIM = 128
MOBA_BLOCK = 256
MOBA_TOPK = 3
ROPE_THETA = 10000.0
LOG2_E = 1.4426950408889634
MASKED = -1e30
GLA_GATE_RANK = 16
GLA_GATE_TAU = 16.0
GLA_CHUNK = 64
GLA_BLOCK_CHUNKS = 4
LANES = 128
VMEM_LIMIT = 60 * 1024 * 1024

_NT = (((1,), (1,)), ((), ()))
_TN = (((0,), (0,)), ((), ()))


def _params(*sem):
    return pltpu.CompilerParams(dimension_semantics=sem, vmem_limit_bytes=VMEM_LIMIT)


def _rms(x, g):
    ms = jnp.mean(x * x, axis=-1, keepdims=True)
    return x * lax.rsqrt(ms + EPS) * g


def _rmsnorm_body(x_ref, g_ref, o_ref):
    o_ref[...] = _rms(x_ref[...], g_ref[...]).astype(o_ref.dtype)


def _rmsnorm(x, g, tm):
    m, d = x.shape
    return pl.pallas_call(
        _rmsnorm_body,
        grid=(m // tm,),
        in_specs=[pl.BlockSpec((tm, d), lambda i: (i, 0)),
                  pl.BlockSpec((1, d), lambda i: (0, 0))],
        out_specs=pl.BlockSpec((tm, d), lambda i: (i, 0)),
        out_shape=jax.ShapeDtypeStruct((m, d), BF16),
        compiler_params=_params("parallel"),
        name="rmsnorm",
    )(x, g)


def _mid_norm_body(y_ref, x_ref, g1_ref, g2_ref, x1_ref, h_ref):
    x1 = x_ref[...] + _rms(y_ref[...].astype(F32), g1_ref[...])
    x1_ref[...] = x1
    h_ref[...] = _rms(x1, g2_ref[...]).astype(h_ref.dtype)


def _mid_norm(y, x, g1, g2, tm):
    m, d = x.shape
    row = pl.BlockSpec((tm, d), lambda i: (i, 0))
    vec = pl.BlockSpec((1, d), lambda i: (0, 0))
    return pl.pallas_call(
        _mid_norm_body,
        grid=(m // tm,),
        in_specs=[row, row, vec, vec],
        out_specs=[row, row],
        out_shape=[jax.ShapeDtypeStruct((m, d), F32), jax.ShapeDtypeStruct((m, d), BF16)],
        compiler_params=_params("parallel"),
        name="mid_norm",
    )(y, x, g1, g2)


def _final_norm_body(y_ref, x_ref, g_ref, o_ref):
    o_ref[...] = x_ref[...] + _rms(y_ref[...].astype(F32), g_ref[...])


def _final_norm(y, x, g, tm):
    m, d = x.shape
    row = pl.BlockSpec((tm, d), lambda i: (i, 0))
    return pl.pallas_call(
        _final_norm_body,
        grid=(m // tm,),
        in_specs=[row, row, pl.BlockSpec((1, d), lambda i: (0, 0))],
        out_specs=row,
        out_shape=jax.ShapeDtypeStruct((m, d), F32),
        compiler_params=_params("parallel"),
        name="final_norm",
    )(y, x, g)


def _tile_copy(w_hbm, stage, sem, jj, *, tn, n, col0, transposed, ragged):
    width = n % tn if ragged else tn
    first = col0 + jj * tn
    if transposed:
        return pltpu.make_async_copy(w_hbm.at[pl.ds(first, width), :],
                                     stage.at[pl.ds(0, width), :], sem)
    return pltpu.make_async_copy(w_hbm.at[:, pl.ds(first, width)],
                                 stage.at[:, pl.ds(0, width)], sem)


def _stage_weight(w_hbm, stage, wbf, sem, *, tn, n, col0=0, transposed=False):
    j = pl.program_id(0)
    nj = pl.num_programs(0)
    copy = functools.partial(_tile_copy, w_hbm, stage, sem, tn=tn, n=n, col0=col0,
                             transposed=transposed)

    def for_tile(jj, act):
        if n % tn:
            pl.when(jj < nj - 1)(lambda: act(copy(jj, ragged=False)))
            pl.when(jj == nj - 1)(lambda: act(copy(jj, ragged=True)))
        else:
            act(copy(jj, ragged=False))

    @pl.when(pl.program_id(1) == 0)
    def _():
        pl.when(j == 0)(lambda: for_tile(j, lambda c: c.start()))
        for_tile(j, lambda c: c.wait())
        wbf[...] = stage[...].astype(BF16)
        pl.when(j + 1 < nj)(lambda: for_tile(j + 1, lambda c: c.start()))


def _weight_scratch(k, tn, transposed=False):
    shape = (tn, k) if transposed else (k, tn)
    return [pltpu.VMEM(shape, F32), pltpu.VMEM(shape, BF16)]


_HBM = pl.BlockSpec(memory_space=pl.ANY)


def _mm_body(x_ref, w_hbm, o_ref, stage, wbf, sem, **tiles):
    _stage_weight(w_hbm, stage, wbf, sem.at[0], **tiles)
    dims = _NT if tiles["transposed"] else (((1,), (0,)), ((), ()))
    o_ref[...] = lax.dot_general(x_ref[...], wbf[...], dims,
                                 preferred_element_type=F32).astype(o_ref.dtype)


def _matmul(x, w, n, tm, tn, out_dtype, name, col0=0, transposed=False):
    m, k = x.shape
    return pl.pallas_call(
        functools.partial(_mm_body, tn=tn, n=n, col0=col0, transposed=transposed),
        grid=(n // tn, m // tm),
        in_specs=[pl.BlockSpec((tm, k), lambda j, i: (i, 0)), _HBM],
        out_specs=pl.BlockSpec((tm, tn), lambda j, i: (i, j)),
        out_shape=jax.ShapeDtypeStruct((m, n), out_dtype),
        scratch_shapes=_weight_scratch(k, tn, transposed) + [pltpu.SemaphoreType.DMA((1,))],
        compiler_params=_params("arbitrary", "arbitrary"),
        name=name,
    )(x, w)


def _mm_small_body(x_ref, w_ref, o_ref):
    o_ref[...] = jnp.dot(x_ref[...], w_ref[...].astype(BF16),
                         preferred_element_type=F32).astype(o_ref.dtype)


def _matmul_small(x, w, tm, out_dtype, name):
    m, k = x.shape
    n = w.shape[1]
    return pl.pallas_call(
        _mm_small_body,
        grid=(m // tm,),
        in_specs=[pl.BlockSpec((tm, k), lambda i: (i, 0)), pl.BlockSpec((k, n), lambda i: (0, 0))],
        out_specs=pl.BlockSpec((tm, n), lambda i: (i, 0)),
        out_shape=jax.ShapeDtypeStruct((m, n), out_dtype),
        compiler_params=_params("parallel"),
        name=name,
    )(x, w)


def _merge_body(om_ref, og_ref, wm_hbm, wg_hbm, gm_ref, gg_ref, o_ref,
                wm_stage, wm_bf, wg_stage, wg_bf, sem, **tiles):
    _stage_weight(wm_hbm, wm_stage, wm_bf, sem.at[0], **tiles)
    _stage_weight(wg_hbm, wg_stage, wg_bf, sem.at[1], **tiles)
    ym = jnp.dot(om_ref[...], wm_bf[...], preferred_element_type=F32)
    yg = jnp.dot(og_ref[...], wg_bf[...], preferred_element_type=F32)
    merged = (jax.nn.sigmoid(gm_ref[...].astype(F32)) * ym
              + jax.nn.sigmoid(gg_ref[...].astype(F32)) * yg)
    o_ref[...] = merged.astype(o_ref.dtype)


def _merge(o_moba, o_gla, w_moba, w_gla, gates, tm, tn):
    m, km = o_moba.shape
    kg = o_gla.shape[1]
    d = w_moba.shape[1]
    nj = d // tn
    return pl.pallas_call(
        functools.partial(_merge_body, tn=tn, n=d),
        grid=(nj, m // tm),
        in_specs=[pl.BlockSpec((tm, km), lambda j, i: (i, 0)),
                  pl.BlockSpec((tm, kg), lambda j, i: (i, 0)),
                  _HBM, _HBM,
                  pl.BlockSpec((tm, tn), lambda j, i: (i, j)),
                  pl.BlockSpec((tm, tn), lambda j, i: (i, j + nj))],
        out_specs=pl.BlockSpec((tm, tn), lambda j, i: (i, j)),
        out_shape=jax.ShapeDtypeStruct((m, d), BF16),
        scratch_shapes=(_weight_scratch(km, tn) + _weight_scratch(kg, tn)
                        + [pltpu.SemaphoreType.DMA((2,))]),
        compiler_params=_params("arbitrary", "arbitrary"),
        name="branch_merge",
    )(o_moba, o_gla, w_moba, w_gla, gates, gates)


def _ffn_up_body(h_ref, wg_hbm, wu_hbm, wd_ref, o_ref, wd_bf_ref,
                 wg_stage, wg_bf, wu_stage, wu_bf, sem, **tiles):
    _stage_weight(wg_hbm, wg_stage, wg_bf, sem.at[0], **tiles)
    _stage_weight(wu_hbm, wu_stage, wu_bf, sem.at[1], **tiles)

    def compute(width):
        h = h_ref[...]
        a = jnp.dot(h, wg_bf[:, :width], preferred_element_type=F32)
        b = jnp.dot(h, wu_bf[:, :width], preferred_element_type=F32)
        o_ref[:, :width] = (a * jax.nn.sigmoid(a) * b).astype(o_ref.dtype)

    tn, short = tiles["tn"], tiles["n"] % tiles["tn"]
    if short:
        last = pl.num_programs(0) - 1
        pl.when(pl.program_id(0) < last)(lambda: compute(tn))
        pl.when(pl.program_id(0) == last)(lambda: compute(short))
    else:
        compute(tn)
    wd_bf_ref[...] = wd_ref[...].astype(BF16)


def _ffn_up(h, w_gate, w_up, w_down, tm, tn):
    m, k = h.shape
    n = w_gate.shape[1]
    ni = m // tm
    steps = pl.cdiv(n, tn) * ni
    slab = -(-pl.cdiv(w_down.shape[0], steps) // 16) * 16
    last = pl.cdiv(w_down.shape[0], slab) - 1
    wd_spec = pl.BlockSpec((slab, w_down.shape[1]), lambda j, i: (jnp.minimum(j * ni + i, last), 0))
    return pl.pallas_call(
        functools.partial(_ffn_up_body, tn=tn, n=n),
        grid=(pl.cdiv(n, tn), ni),
        in_specs=[pl.BlockSpec((tm, k), lambda j, i: (i, 0)), _HBM, _HBM, wd_spec],
        out_specs=[pl.BlockSpec((tm, tn), lambda j, i: (i, j)), wd_spec],
        out_shape=[jax.ShapeDtypeStruct((m, n), BF16), jax.ShapeDtypeStruct(w_down.shape, BF16)],
        scratch_shapes=(_weight_scratch(k, tn) + _weight_scratch(k, tn)
                        + [pltpu.SemaphoreType.DMA((2,))]),
        compiler_params=_params("arbitrary", "arbitrary"),
        name="ffn_up",
    )(h, w_gate, w_up, w_down)


def _mm_bf16_body(x_ref, w_ref, o_ref):
    o_ref[...] = jnp.dot(x_ref[...], w_ref[...], preferred_element_type=F32).astype(o_ref.dtype)


def _matmul_bf16(x, w, tm, tn, out_dtype, name):
    m, k = x.shape
    n = w.shape[1]
    return pl.pallas_call(
        _mm_bf16_body,
        grid=(n // tn, m // tm),
        in_specs=[pl.BlockSpec((tm, k), lambda j, i: (i, 0)),
                  pl.BlockSpec((k, tn), lambda j, i: (0, j))],
        out_specs=pl.BlockSpec((tm, tn), lambda j, i: (i, j)),
        out_shape=jax.ShapeDtypeStruct((m, n), out_dtype),
        compiler_params=_params("parallel", "parallel"),
        name=name,
    )(x, w)


def _moba_setup(q_ref, k_ref, v_ref, cos_ref, sin_ref, o_ref, *, nb, heads):
    blk = MOBA_BLOCK
    hd = MOBA_HEAD_DIM
    seq = q_ref.shape[1]
    scale = hd ** -0.5
    cos = cos_ref[...]
    sin = sin_ref[...]
    row = lax.broadcasted_iota(jnp.int32, (blk, blk), 0)
    col = lax.broadcasted_iota(jnp.int32, (blk, blk), 1)
    causal = col <= row

    r_l = lax.broadcasted_iota(jnp.int32, (LANES, LANES), 0)
    c_l = lax.broadcasted_iota(jnp.int32, (LANES, LANES), 1)
    sum_wins = jnp.where((c_l == r_l % nb) & (r_l < nb * nb), 1.0, 0.0).astype(BF16)
    key_lane = lax.broadcasted_iota(jnp.int32, (seq, LANES), 1)
    key_block = lax.broadcasted_iota(jnp.int32, (seq, LANES), 0) // blk
    block_ind = jnp.where(key_lane == key_block, 1.0, 0.0).astype(BF16)
    ones_col = jnp.where(key_lane == 0, 1.0, 0.0).astype(BF16)

    lane = lax.broadcasted_iota(jnp.int32, (blk, LANES), 1)
    lane_j = lane // nb
    lane_n = lane % nb
    lane_m = (lane_n - lane_j + nb) % nb
    tie = jnp.where(lane_m < lane_n, 1.0, 0.0)

    def rope(t):
        t = t.astype(F32)
        return t * cos + pltpu.roll(t, hd // 2, axis=1) * sin

    def prepare(h):
        cols = slice(h * hd, (h + 1) * hd)
        qr = rope(q_ref[0, :, cols])
        kr = rope(k_ref[0, :, cols])
        kmean = jnp.mean(kr.reshape(nb, blk, hd), axis=1)
        parts = [kmean] + [jnp.roll(kmean, j, axis=0) - kmean for j in range(1, nb)]
        parts.append(jnp.zeros((LANES - nb * nb, hd), F32))
        diff = jnp.concatenate(parts, axis=0)
        q_hi = qr.astype(BF16)
        q_lo = (qr - q_hi.astype(F32)).astype(BF16)
        d_hi = diff.astype(BF16)
        d_lo = (diff - d_hi.astype(F32)).astype(BF16)
        gate = lax.dot_general(jnp.concatenate([q_hi, q_lo, q_hi], axis=1),
                               jnp.concatenate([d_hi, d_hi, d_lo], axis=1), _NT,
                               preferred_element_type=F32)
        return dict(
            cols=cols, gate=gate,
            q=(qr * (scale * LOG2_E)).astype(BF16),
            k=jnp.concatenate([kr.astype(BF16), block_ind], axis=1),
            v=jnp.concatenate([v_ref[0, :, cols], ones_col], axis=1))

    def block_bias(gate, i):
        finite = jnp.abs(gate) < jnp.inf
        if i <= MOBA_TOPK:
            keep = finite
        else:
            compared = (lane_j >= 1) & (lane_j < nb) & (lane_n < i) & (lane_m < i)
            wins = jnp.where(compared, jnp.where(gate > 0.0, 1.0, jnp.where(gate == 0.0, tie, 0.0)), 0.0)
            wins = jnp.where(lane < nb, jnp.where(finite, 0.0, float(MOBA_TOPK)), wins)
            rank = jnp.dot(wins.astype(BF16), sum_wins, preferred_element_type=F32)
            keep = rank < MOBA_TOPK
        return jnp.where(lane < i, jnp.where(keep, 0.0, MASKED), jnp.where(lane == i, 0.0, MASKED))

    def probabilities(s, i):
        own = jnp.where(causal, s[:, i * blk:], -jnp.inf)
        s = jnp.concatenate([s[:, :i * blk], own], axis=1) if i else own
        m = jnp.max(s, axis=1, keepdims=True)
        return jnp.exp2(s - m).astype(BF16)

    hs = [prepare(h) for h in range(heads)]

    def scores(i):
        rows = slice(i * blk, (i + 1) * blk)
        kv = slice(0, (i + 1) * blk)
        q_aug = [jnp.concatenate([h["q"][rows], block_bias(h["gate"][rows], i).astype(BF16)], axis=1)
                 for h in hs]
        return [lax.dot_general(q_aug[n], h["k"][kv], _NT, preferred_element_type=F32)
                for n, h in enumerate(hs)]

    def finish(i, s):
        rows = slice(i * blk, (i + 1) * blk)
        kv = slice(0, (i + 1) * blk)
        p = [probabilities(s[n], i) for n in range(heads)]
        for n, h in enumerate(hs):
            o = jnp.dot(p[n], h["v"][kv], preferred_element_type=F32)
            o_ref[0, rows, h["cols"]] = (o[:, :hd] / o[:, hd:hd + 1]).astype(o_ref.dtype)

    return scores, finish


def _split2(x):
    hi = x.astype(BF16)
    return hi, (x - hi.astype(F32)).astype(BF16)


def _gla_setup(q_ref, k_ref, v_ref, gr_ref, ga_ref, up_ref, bias_ref, g_ref, o_ref,
               st_ref, la_ref, *, dk):
    c = GLA_CHUNK
    nbc = GLA_BLOCK_CHUNKS
    mid = nbc // 2
    blk = c * nbc
    scale = dk ** -0.5
    row = lax.broadcasted_iota(jnp.int32, (blk, blk), 0)
    col = lax.broadcasted_iota(jnp.int32, (blk, blk), 1)
    shift = c.bit_length() - 1
    row_c = jnp.right_shift(row, shift)
    col_c = jnp.right_shift(col, shift)
    below = row_c > col_c
    diag = (row_c == col_c) & (col <= row)
    tri = jnp.where(diag, 1.0, 0.0).astype(BF16)

    rank = GLA_GATE_RANK
    g = ga_ref[0]
    g_hi = g.astype(BF16).astype(F32)
    lane = lax.broadcasted_iota(jnp.int32, g.shape, 1)
    lhs = jnp.where((lane >= rank) & (lane < 2 * rank), g - g_hi, g_hi).astype(BF16)
    u = up_ref[...]
    u_hi = u.astype(BF16).astype(F32)
    rhs = jnp.where(lax.broadcasted_iota(jnp.int32, u.shape, 0) >= 2 * rank, u - u_hi,
                    u_hi).astype(BF16)
    z = jnp.dot(lhs, rhs, preferred_element_type=F32) + bias_ref[...]
    la_ref[...] = (jnp.minimum(z, 0.0) - jnp.log(1.0 + jnp.exp(-jnp.abs(z)))) / GLA_GATE_TAU
    st_ref[...] = jnp.zeros_like(st_ref)

    def stack(parts):
        return jnp.concatenate(parts, axis=0).astype(BF16)

    def prepare(bi):
        rows = slice(bi * blk, (bi + 1) * blk)
        hi, lo = _split2(la_ref[rows, :])
        b = (jnp.dot(tri, hi, preferred_element_type=F32)
             + jnp.dot(tri, lo, preferred_element_type=F32))
        q = q_ref[0, rows, :].astype(F32) * scale
        k = k_ref[0, rows, :].astype(F32)
        q_dec = q * jnp.exp(b)
        k_inv = k * jnp.exp(-b)
        tot = [b[(j + 1) * c - 1:(j + 1) * c, :] for j in range(nbc)]
        pre = [jnp.zeros_like(tot[0])]
        for j in range(nbc):
            pre.append(pre[-1] + tot[j])
        q_mid, k_mid, q_start, k_end = [], [], [], []
        for j in range(nbc):
            sl = slice(j * c, (j + 1) * c)
            q_mid.append(q_dec[sl] * jnp.exp(pre[j] - pre[mid]) if j else q_dec[sl])
            k_mid.append(k_inv[sl] * jnp.exp(pre[mid] - pre[j]) if j < nbc - 1 else k_inv[sl])
            q_start.append(q_dec[sl] * jnp.exp(pre[j]))
            k_end.append(k[sl] * jnp.exp((pre[nbc] - pre[j]) - b[sl]))
        a_diag = lax.dot_general(q_dec.astype(BF16), k_inv.astype(BF16), _NT,
                                 preferred_element_type=F32)
        a_off = lax.dot_general(stack(q_mid), stack(k_mid), _NT, preferred_element_type=F32)
        attn = jnp.where(below, a_off, jnp.where(diag, a_diag, 0.0)).astype(BF16)
        return attn, stack(q_start), stack(k_end), jnp.exp(pre[nbc])

    def advance(bi, prepared):
        attn, q_start, k_end, decay = prepared
        rows = slice(bi * blk, (bi + 1) * blk)
        v = v_ref[0, rows, :]
        st = st_ref[...]
        o = (jnp.dot(attn, v, preferred_element_type=F32)
             + lax.dot_general(q_start, st.astype(BF16), _NT, preferred_element_type=F32))
        st_ref[...] = st * decay + lax.dot_general(v, k_end, _TN, preferred_element_type=F32)
        gr = gr_ref[0, rows, :].astype(F32)
        y = _rms(o, g_ref[...]) * (gr * jax.nn.sigmoid(gr))
        o_ref[0, rows, :] = y.astype(o_ref.dtype)

    return prepare, advance


def _mixers_body(mq_ref, mk_ref, mv_ref, cos_ref, sin_ref, gq_ref, gk_ref, gv_ref, gr_ref, ga_ref,
                 up_ref, bias_ref, g_ref, om_ref, og_ref, st_ref, la_ref, *, nb, moba_heads, dk):
    moba_scores, moba_finish = _moba_setup(mq_ref, mk_ref, mv_ref, cos_ref, sin_ref, om_ref,
                                           nb=nb, heads=moba_heads)
    gla_prepare, gla_advance = _gla_setup(gq_ref, gk_ref, gv_ref, gr_ref, ga_ref, up_ref, bias_ref,
                                          g_ref, og_ref, st_ref, la_ref, dk=dk)
    s_next = moba_scores(0)
    for i in range(nb):
        s_now, s_next = s_next, (moba_scores(i + 1) if i + 1 < nb else None)
        gla_advance(i, gla_prepare(i))
        moba_finish(i, s_now)


def _mixers(u, cos, sin, ga, up, bias, g, batch, seq, moba_heads, gla_heads, dk, dv,
            q_off, k_off, v_off, r_off):
    hd = MOBA_HEAD_DIM
    assert MOBA_BLOCK == GLA_CHUNK * GLA_BLOCK_CHUNKS and moba_heads % gla_heads == 0
    per = moba_heads // gla_heads
    nb = seq // MOBA_BLOCK
    mb = lambda off: pl.BlockSpec((1, seq, per * hd), lambda b, h: (b, 0, h + off))
    qk = lambda off: pl.BlockSpec((1, seq, dk), lambda b, h: (b, 0, h + off))
    vv = lambda off: pl.BlockSpec((1, seq, dv), lambda b, h: (b, 0, h + off))
    tab = pl.BlockSpec((seq, hd), lambda b, h: (0, 0))
    return pl.pallas_call(
        functools.partial(_mixers_body, nb=nb, moba_heads=per, dk=dk),
        grid=(batch, gla_heads),
        in_specs=[mb(0), mb(gla_heads), mb(2 * gla_heads), tab, tab,
                  qk(q_off), qk(k_off), vv(v_off), vv(r_off),
                  pl.BlockSpec((1, seq, LANES), lambda b, h: (b, 0, 0)),
                  pl.BlockSpec((LANES, dk), lambda b, h: (0, h)),
                  pl.BlockSpec((1, dk), lambda b, h: (0, h)),
                  pl.BlockSpec((1, dv), lambda b, h: (0, 0))],
        out_specs=[pl.BlockSpec((1, seq, per * hd), lambda b, h: (b, 0, h)),
                   pl.BlockSpec((1, seq, dv), lambda b, h: (b, 0, h))],
        out_shape=[jax.ShapeDtypeStruct((batch, seq, moba_heads * hd), BF16),
                   jax.ShapeDtypeStruct((batch, seq, gla_heads * dv), BF16)],
        scratch_shapes=[pltpu.VMEM((dv, dk), F32),
                        pltpu.VMEM((seq, dk), F32)],
        compiler_params=_params("parallel", "parallel"),
        name="token_mixers",
    )(u, u, u, cos, sin, u, u, u, u, ga, up, bias, g)


def _tile(n, pref):
    return pref if n % pref == 0 else n


def _layer(x, pre_mix_g, w_in, gate_up, gate_bias, gla_norm_g, w_moba, w_gla, w_out,
           post_mix_g, pre_ffn_g, w_fg, w_fu, w_fd, post_ffn_g):
    batch, seq, d = x.shape
    t = batch * seq
    half = d // 2
    moba_heads = half // MOBA_HEAD_DIM
    gla_heads = d // 512
    dk = half // gla_heads
    dv = d // gla_heads
    n_main = 9 * half
    n_gate0 = n_main + GLA_GATE_RANK
    d_ff = w_fg.shape[1]

    tm = _tile(t, 1024)
    tn = _tile(d, 1024)
    tr = _tile(t, 512)
    x2 = x.reshape(t, d)
    vec = lambda g: g.reshape(1, -1)

    w_in_t = w_in.T
    w_ga = jnp.pad(jnp.tile(w_in_t[n_main:n_gate0], (3, 1)),
                   ((0, LANES - 3 * GLA_GATE_RANK), (0, 0))).T
    up_pad = jnp.pad(jnp.tile(gate_up, (3, 1)), ((0, LANES - 3 * GLA_GATE_RANK), (0, 0)))

    hd2 = MOBA_HEAD_DIM // 2
    inv_freq = ROPE_THETA ** (-jnp.arange(hd2, dtype=F32) / hd2)
    ang = jnp.arange(seq, dtype=F32)[:, None] * inv_freq[None, :]
    cos = jnp.concatenate([jnp.cos(ang), jnp.cos(ang)], axis=1)
    sin = jnp.concatenate([-jnp.sin(ang), jnp.sin(ang)], axis=1)

    h = _rmsnorm(x2, vec(pre_mix_g), tr)
    u = _matmul(h, w_in_t, n_main, tm, 1024 if n_main % 1024 == 0 else 512, BF16, "in_proj_main",
                transposed=True)
    gates = _matmul(h, w_in_t, 2 * d, tm, tn, BF16, "in_proj_gates", col0=n_gate0, transposed=True)
    ga = _matmul_small(h, w_ga, tm, F32, "in_proj_forget")

    u3 = u.reshape(batch, seq, n_main)
    o_moba, o_gla = _mixers(u3, cos, sin, ga.reshape(batch, seq, LANES), up_pad, vec(gate_bias),
                            vec(gla_norm_g), batch, seq, moba_heads, gla_heads, dk, dv,
                            q_off=3 * half // dk, k_off=4 * half // dk, v_off=5 * half // dv,
                            r_off=7 * half // dv)

    merged = _merge(o_moba.reshape(t, half), o_gla.reshape(t, d), w_moba, w_gla, gates, tm, 512)
    y = _matmul(merged, w_out, d, tm, tn, BF16, "out_proj")
    x1, h2 = _mid_norm(y, x2, vec(post_mix_g), vec(pre_ffn_g), tr)

    a, w_fd_bf = _ffn_up(h2, w_fg, w_fu, w_fd, tm, 512)
    y2 = _matmul_bf16(a, w_fd_bf, _tile(t, 512), 512, BF16, "ffn_down")
    out = _final_norm(y2, x1, vec(post_ffn_g), tr)
    return out.reshape(batch, seq, d)


def kernel(x, pre_mix_norm_g, w_in, gla_gate_up, gla_gate_bias, gla_out_norm_g, w_branch_moba,
           w_branch_gla, w_out, post_mix_norm_g, pre_ffn_norm_g, w_ffn_gate, w_ffn_up, w_ffn_down,
           post_ffn_norm_g):
    for layer in range(w_in.shape[0]):
        x = _layer(x, pre_mix_norm_g[layer], w_in[layer], gla_gate_up[layer], gla_gate_bias[layer],
                   gla_out_norm_g[layer], w_branch_moba[layer], w_branch_gla[layer], w_out[layer],
                   post_mix_norm_g[layer], pre_ffn_norm_g[layer], w_ffn_gate[layer],
                   w_ffn_up[layer], w_ffn_down[layer], post_ffn_norm_g[layer])
    return x
```

```python
import functools

import jax
import jax.numpy as jnp
from jax import lax
from jax.experimental import pallas as pl
from jax.experimental.pallas import tpu as pltpu

F32 = jnp.float32
BF16 = jnp.bfloat16

EPS = 1e-6
MOBA_HEAD_DIM = 128
MOBA_BLOCK = 256
MOBA_TOPK = 3
ROPE_THETA = 10000.0
LOG2_E = 1.4426950408889634
MASKED = -1e30
GLA_GATE_RANK = 16
GLA_GATE_TAU = 16.0
GLA_CHUNK = 64
GLA_BLOCK_CHUNKS = 4
LANES = 128
VMEM_LIMIT = 60 * 1024 * 1024

_NT = (((1,), (1,)), ((), ()))
_TN = (((0,), (0,)), ((), ()))


def _params(*sem):
    return pltpu.CompilerParams(dimension_semantics=sem, vmem_limit_bytes=VMEM_LIMIT)


def _rms(x, g):
    ms = jnp.mean(x * x, axis=-1, keepdims=True)
    return x * lax.rsqrt(ms + EPS) * g


def _rmsnorm_proj_body(x_ref, g_ref, w_ref, o_ref, p_ref):
    h = _rms(x_ref[...], g_ref[...]).astype(o_ref.dtype)
    o_ref[...] = h
    p_ref[...] = jnp.dot(h, w_ref[...].astype(BF16), preferred_element_type=F32)


def _rmsnorm_proj(x, g, w, tm):
    m, d = x.shape
    n = w.shape[1]
    return pl.pallas_call(
        _rmsnorm_proj_body,
        grid=(m // tm,),
        in_specs=[pl.BlockSpec((tm, d), lambda i: (i, 0)),
                  pl.BlockSpec((1, d), lambda i: (0, 0)),
                  pl.BlockSpec((d, n), lambda i: (0, 0))],
        out_specs=[pl.BlockSpec((tm, d), lambda i: (i, 0)), pl.BlockSpec((tm, n), lambda i: (i, 0))],
        out_shape=[jax.ShapeDtypeStruct((m, d), BF16), jax.ShapeDtypeStruct((m, n), F32)],
        compiler_params=_params("parallel"),
        name="rmsnorm_forget_proj",
    )(x, g, w)


def _mid_norm_body(y_ref, x_ref, g1_ref, g2_ref, x1_ref, h_ref):
    x1 = x_ref[...] + _rms(y_ref[...].astype(F32), g1_ref[...])
    x1_ref[...] = x1
    h_ref[...] = _rms(x1, g2_ref[...]).astype(h_ref.dtype)


def _mid_norm(y, x, g1, g2, tm):
    m, d = x.shape
    row = pl.BlockSpec((tm, d), lambda i: (i, 0))
    vec = pl.BlockSpec((1, d), lambda i: (0, 0))
    return pl.pallas_call(
        _mid_norm_body,
        grid=(m // tm,),
        in_specs=[row, row, vec, vec],
        out_specs=[row, row],
        out_shape=[jax.ShapeDtypeStruct((m, d), F32), jax.ShapeDtypeStruct((m, d), BF16)],
        compiler_params=_params("parallel"),
        name="mid_norm",
    )(y, x, g1, g2)


def _final_norm_body(y_ref, x_ref, g_ref, o_ref):
    o_ref[...] = x_ref[...] + _rms(y_ref[...].astype(F32), g_ref[...])


def _final_norm(y, x, g, tm):
    m, d = x.shape
    row = pl.BlockSpec((tm, d), lambda i: (i, 0))
    return pl.pallas_call(
        _final_norm_body,
        grid=(m // tm,),
        in_specs=[row, row, pl.BlockSpec((1, d), lambda i: (0, 0))],
        out_specs=row,
        out_shape=jax.ShapeDtypeStruct((m, d), F32),
        compiler_params=_params("parallel"),
        name="final_norm",
    )(y, x, g)


def _tile_copy(w_hbm, stage, sem, jj, *, tn, n, col0, transposed, ragged):
    width = n % tn if ragged else tn
    first = col0 + jj * tn
    if transposed:
        return pltpu.make_async_copy(w_hbm.at[pl.ds(first, width), :],
                                     stage.at[pl.ds(0, width), :], sem)
    return pltpu.make_async_copy(w_hbm.at[:, pl.ds(first, width)],
                                 stage.at[:, pl.ds(0, width)], sem)


def _stage_weight(w_hbm, stage, wbf, sem, *, tn, n, col0=0, transposed=False):
    j = pl.program_id(0)
    nj = pl.num_programs(0)
    copy = functools.partial(_tile_copy, w_hbm, stage, sem, tn=tn, n=n, col0=col0,
                             transposed=transposed)

    def for_tile(jj, act):
        if n % tn:
            pl.when(jj < nj - 1)(lambda: act(copy(jj, ragged=False)))
            pl.when(jj == nj - 1)(lambda: act(copy(jj, ragged=True)))
        else:
            act(copy(jj, ragged=False))

    @pl.when(pl.program_id(1) == 0)
    def _():
        pl.when(j == 0)(lambda: for_tile(j, lambda c: c.start()))
        for_tile(j, lambda c: c.wait())
        wbf[...] = stage[...].astype(BF16)
        pl.when(j + 1 < nj)(lambda: for_tile(j + 1, lambda c: c.start()))


def _weight_scratch(k, tn, transposed=False):
    shape = (tn, k) if transposed else (k, tn)
    return [pltpu.VMEM(shape, F32), pltpu.VMEM(shape, BF16)]


_HBM = pl.BlockSpec(memory_space=pl.ANY)


def _mm_body(x_ref, w_hbm, o_ref, stage, wbf, sem, **tiles):
    _stage_weight(w_hbm, stage, wbf, sem.at[0], **tiles)
    dims = _NT if tiles["transposed"] else (((1,), (0,)), ((), ()))
    o_ref[...] = lax.dot_general(x_ref[...], wbf[...], dims,
                                 preferred_element_type=F32).astype(o_ref.dtype)


def _matmul(x, w, n, tm, tn, out_dtype, name, col0=0, transposed=False):
    m, k = x.shape
    return pl.pallas_call(
        functools.partial(_mm_body, tn=tn, n=n, col0=col0, transposed=transposed),
        grid=(n // tn, m // tm),
        in_specs=[pl.BlockSpec((tm, k), lambda j, i: (i, 0)), _HBM],
        out_specs=pl.BlockSpec((tm, tn), lambda j, i: (i, j)),
        out_shape=jax.ShapeDtypeStruct((m, n), out_dtype),
        scratch_shapes=_weight_scratch(k, tn, transposed) + [pltpu.SemaphoreType.DMA((1,))],
        compiler_params=_params("arbitrary", "arbitrary"),
        name=name,
    )(x, w)


def _merge_body(om_ref, og_ref, wm_hbm, wg_hbm, gm_ref, gg_ref, o_ref,
                wm_stage, wm_bf, wg_stage, wg_bf, sem, **tiles):
    _stage_weight(wm_hbm, wm_stage, wm_bf, sem.at[0], **tiles)
    _stage_weight(wg_hbm, wg_stage, wg_bf, sem.at[1], **tiles)
    ym = jnp.dot(om_ref[...], wm_bf[...], preferred_element_type=F32)
    yg = jnp.dot(og_ref[...], wg_bf[...], preferred_element_type=F32)
    merged = (jax.nn.sigmoid(gm_ref[...].astype(F32)) * ym
              + jax.nn.sigmoid(gg_ref[...].astype(F32)) * yg)
    o_ref[...] = merged.astype(o_ref.dtype)


def _merge(o_moba, o_gla, w_moba, w_gla, gates, tm, tn):
    m, km = o_moba.shape
    kg = o_gla.shape[1]
    d = w_moba.shape[1]
    nj = d // tn
    return pl.pallas_call(
        functools.partial(_merge_body, tn=tn, n=d),
        grid=(nj, m // tm),
        in_specs=[pl.BlockSpec((tm, km), lambda j, i: (i, 0)),
                  pl.BlockSpec((tm, kg), lambda j, i: (i, 0)),
                  _HBM, _HBM,
                  pl.BlockSpec((tm, tn), lambda j, i: (i, j)),
                  pl.BlockSpec((tm, tn), lambda j, i: (i, j + nj))],
        out_specs=pl.BlockSpec((tm, tn), lambda j, i: (i, j)),
        out_shape=jax.ShapeDtypeStruct((m, d), BF16),
        scratch_shapes=(_weight_scratch(km, tn) + _weight_scratch(kg, tn)
                        + [pltpu.SemaphoreType.DMA((2,))]),
        compiler_params=_params("arbitrary", "arbitrary"),
        name="branch_merge",
    )(o_moba, o_gla, w_moba, w_gla, gates, gates)


def _ffn_up_body(h_ref, wg_hbm, wu_hbm, wd_ref, o_ref, wd_bf_ref,
                 wg_stage, wg_bf, wu_stage, wu_bf, sem, **tiles):
    _stage_weight(wg_hbm, wg_stage, wg_bf, sem.at[0], **tiles)
    _stage_weight(wu_hbm, wu_stage, wu_bf, sem.at[1], **tiles)

    def compute(width):
        h = h_ref[...]
        a = jnp.dot(h, wg_bf[:, :width], preferred_element_type=F32)
        b = jnp.dot(h, wu_bf[:, :width], preferred_element_type=F32)
        o_ref[:, :width] = (a * jax.nn.sigmoid(a) * b).astype(o_ref.dtype)

    tn, short = tiles["tn"], tiles["n"] % tiles["tn"]
    if short:
        last = pl.num_programs(0) - 1
        pl.when(pl.program_id(0) < last)(lambda: compute(tn))
        pl.when(pl.program_id(0) == last)(lambda: compute(short))
    else:
        compute(tn)
    wd_bf_ref[...] = wd_ref[...].astype(BF16)


def _ffn_up(h, w_gate, w_up, w_down, tm, tn):
    m, k = h.shape
    n = w_gate.shape[1]
    ni = m // tm
    steps = pl.cdiv(n, tn) * ni
    slab = -(-pl.cdiv(w_down.shape[0], steps) // 16) * 16
    last = pl.cdiv(w_down.shape[0], slab) - 1
    wd_spec = pl.BlockSpec((slab, w_down.shape[1]), lambda j, i: (jnp.minimum(j * ni + i, last), 0))
    return pl.pallas_call(
        functools.partial(_ffn_up_body, tn=tn, n=n),
        grid=(pl.cdiv(n, tn), ni),
        in_specs=[pl.BlockSpec((tm, k), lambda j, i: (i, 0)), _HBM, _HBM, wd_spec],
        out_specs=[pl.BlockSpec((tm, tn), lambda j, i: (i, j)), wd_spec],
        out_shape=[jax.ShapeDtypeStruct((m, n), BF16), jax.ShapeDtypeStruct(w_down.shape, BF16)],
        scratch_shapes=(_weight_scratch(k, tn) + _weight_scratch(k, tn)
                        + [pltpu.SemaphoreType.DMA((2,))]),
        compiler_params=_params("arbitrary", "arbitrary"),
        name="ffn_up",
    )(h, w_gate, w_up, w_down)


def _mm_bf16_body(x_ref, w_ref, o_ref):
    o_ref[...] = jnp.dot(x_ref[...], w_ref[...], preferred_element_type=F32).astype(o_ref.dtype)


def _matmul_bf16(x, w, tm, tn, out_dtype, name):
    m, k = x.shape
    n = w.shape[1]
    return pl.pallas_call(
        _mm_bf16_body,
        grid=(n // tn, m // tm),
        in_specs=[pl.BlockSpec((tm, k), lambda j, i: (i, 0)),
                  pl.BlockSpec((k, tn), lambda j, i: (0, j))],
        out_specs=pl.BlockSpec((tm, tn), lambda j, i: (i, j)),
        out_shape=jax.ShapeDtypeStruct((m, n), out_dtype),
        compiler_params=_params("parallel", "parallel"),
        name=name,
    )(x, w)


def _moba_setup(q_ref, k_ref, v_ref, cos_ref, sin_ref, o_ref, *, nb, heads):
    blk = MOBA_BLOCK
    hd = MOBA_HEAD_DIM
    seq = q_ref.shape[1]
    scale = hd ** -0.5
    cos = cos_ref[...]
    sin = sin_ref[...]
    row = lax.broadcasted_iota(jnp.int32, (blk, blk), 0)
    col = lax.broadcasted_iota(jnp.int32, (blk, blk), 1)
    causal = col <= row

    r_l = lax.broadcasted_iota(jnp.int32, (LANES, LANES), 0)
    c_l = lax.broadcasted_iota(jnp.int32, (LANES, LANES), 1)
    sum_wins = jnp.where((c_l == r_l % nb) & (r_l < nb * nb), 1.0, 0.0).astype(BF16)
    key_lane = lax.broadcasted_iota(jnp.int32, (seq, LANES), 1)
    key_block = lax.broadcasted_iota(jnp.int32, (seq, LANES), 0) // blk
    block_ind = jnp.where(key_lane == key_block, 1.0, 0.0).astype(BF16)
    ones_col = jnp.where(key_lane == 0, 1.0, 0.0).astype(BF16)

    lane = lax.broadcasted_iota(jnp.int32, (blk, LANES), 1)
    lane_j = lane // nb
    lane_n = lane % nb
    lane_m = (lane_n - lane_j + nb) % nb
    tie = jnp.where(lane_m < lane_n, 1.0, 0.0)

    def rope(t):
        t = t.astype(F32)
        return t * cos + pltpu.roll(t, hd // 2, axis=1) * sin

    def prepare(h):
        cols = slice(h * hd, (h + 1) * hd)
        qr = rope(q_ref[0, :, cols])
        kr = rope(k_ref[0, :, cols])
        kmean = jnp.mean(kr.reshape(nb, blk, hd), axis=1)
        parts = [kmean] + [jnp.roll(kmean, j, axis=0) - kmean for j in range(1, nb)]
        parts.append(jnp.zeros((LANES - nb * nb, hd), F32))
        diff = jnp.concatenate(parts, axis=0)
        q_hi = qr.astype(BF16)
        q_lo = (qr - q_hi.astype(F32)).astype(BF16)
        d_hi = diff.astype(BF16)
        d_lo = (diff - d_hi.astype(F32)).astype(BF16)
        gate = lax.dot_general(jnp.concatenate([q_hi, q_lo, q_hi], axis=1),
                               jnp.concatenate([d_hi, d_hi, d_lo], axis=1), _NT,
                               preferred_element_type=F32)
        return dict(
            cols=cols, gate=gate,
            q=(qr * (scale * LOG2_E)).astype(BF16),
            k=jnp.concatenate([kr.astype(BF16), block_ind], axis=1),
            v=jnp.concatenate([v_ref[0, :, cols], ones_col], axis=1))

    def block_bias(gate, i):
        finite = jnp.abs(gate) < jnp.inf
        if i <= MOBA_TOPK:
            keep = finite
        else:
            compared = (lane_j >= 1) & (lane_j < nb) & (lane_n < i) & (lane_m < i)
            wins = jnp.where(compared, jnp.where(gate > 0.0, 1.0, jnp.where(gate == 0.0, tie, 0.0)), 0.0)
            wins = jnp.where(lane < nb, jnp.where(finite, 0.0, float(MOBA_TOPK)), wins)
            rank = jnp.dot(wins.astype(BF16), sum_wins, preferred_element_type=F32)
            keep = rank < MOBA_TOPK
        return jnp.where(lane < i, jnp.where(keep, 0.0, MASKED), jnp.where(lane == i, 0.0, MASKED))

    def probabilities(s, i):
        own = jnp.where(causal, s[:, i * blk:], -jnp.inf)
        s = jnp.concatenate([s[:, :i * blk], own], axis=1) if i else own
        m = jnp.max(s, axis=1, keepdims=True)
        return jnp.exp2(s - m).astype(BF16)

    hs = [prepare(h) for h in range(heads)]

    def scores(i):
        rows = slice(i * blk, (i + 1) * blk)
        kv = slice(0, (i + 1) * blk)
        q_aug = [jnp.concatenate([h["q"][rows], block_bias(h["gate"][rows], i).astype(BF16)], axis=1)
                 for h in hs]
        return [lax.dot_general(q_aug[n], h["k"][kv], _NT, preferred_element_type=F32)
                for n, h in enumerate(hs)]

    def finish(i, s):
        rows = slice(i * blk, (i + 1) * blk)
        kv = slice(0, (i + 1) * blk)
        p = [probabilities(s[n], i) for n in range(heads)]
        for n, h in enumerate(hs):
            o = jnp.dot(p[n], h["v"][kv], preferred_element_type=F32)
            o_ref[0, rows, h["cols"]] = (o[:, :hd] / o[:, hd:hd + 1]).astype(o_ref.dtype)

    return scores, finish


def _split2(x):
    hi = x.astype(BF16)
    return hi, (x - hi.astype(F32)).astype(BF16)


def _gla_setup(q_ref, k_ref, v_ref, gr_ref, ga_ref, up_ref, bias_ref, g_ref, o_ref,
               st_ref, la_ref, *, dk):
    c = GLA_CHUNK
    nbc = GLA_BLOCK_CHUNKS
    mid = nbc // 2
    blk = c * nbc
    scale = dk ** -0.5
    row = lax.broadcasted_iota(jnp.int32, (blk, blk), 0)
    col = lax.broadcasted_iota(jnp.int32, (blk, blk), 1)
    shift = c.bit_length() - 1
    row_c = jnp.right_shift(row, shift)
    col_c = jnp.right_shift(col, shift)
    below = row_c > col_c
    diag = (row_c == col_c) & (col <= row)
    tri = jnp.where(diag, 1.0, 0.0).astype(BF16)

    rank = GLA_GATE_RANK
    g = ga_ref[0]
    g_hi = g.astype(BF16).astype(F32)
    lane = lax.broadcasted_iota(jnp.int32, g.shape, 1)
    lhs = jnp.where((lane >= rank) & (lane < 2 * rank), g - g_hi, g_hi).astype(BF16)
    u = up_ref[...]
    u_hi = u.astype(BF16).astype(F32)
    rhs = jnp.where(lax.broadcasted_iota(jnp.int32, u.shape, 0) >= 2 * rank, u - u_hi,
                    u_hi).astype(BF16)
    z = jnp.dot(lhs, rhs, preferred_element_type=F32) + bias_ref[...]
    la_ref[...] = (jnp.minimum(z, 0.0) - jnp.log(1.0 + jnp.exp(-jnp.abs(z)))) / GLA_GATE_TAU
    st_ref[...] = jnp.zeros_like(st_ref)

    def stack(parts):
        return jnp.concatenate(parts, axis=0).astype(BF16)

    def prepare(bi):
        rows = slice(bi * blk, (bi + 1) * blk)
        hi, lo = _split2(la_ref[rows, :])
        b = (jnp.dot(tri, hi, preferred_element_type=F32)
             + jnp.dot(tri, lo, preferred_element_type=F32))
        q = q_ref[0, rows, :].astype(F32) * scale
        k = k_ref[0, rows, :].astype(F32)
        q_dec = q * jnp.exp(b)
        k_inv = k * jnp.exp(-b)
        tot = [b[(j + 1) * c - 1:(j + 1) * c, :] for j in range(nbc)]
        pre = [jnp.zeros_like(tot[0])]
        for j in range(nbc):
            pre.append(pre[-1] + tot[j])
        q_mid, k_mid, q_start, k_end = [], [], [], []
        for j in range(nbc):
            sl = slice(j * c, (j + 1) * c)
            q_mid.append(q_dec[sl] * jnp.exp(pre[j] - pre[mid]) if j else q_dec[sl])
            k_mid.append(k_inv[sl] * jnp.exp(pre[mid] - pre[j]) if j < nbc - 1 else k_inv[sl])
            q_start.append(q_dec[sl] * jnp.exp(pre[j]))
            k_end.append(k[sl] * jnp.exp((pre[nbc] - pre[j]) - b[sl]))
        a_diag = lax.dot_general(q_dec.astype(BF16), k_inv.astype(BF16), _NT,
                                 preferred_element_type=F32)
        a_off = lax.dot_general(stack(q_mid), stack(k_mid), _NT, preferred_element_type=F32)
        attn = jnp.where(below, a_off, jnp.where(diag, a_diag, 0.0)).astype(BF16)
        return attn, stack(q_start), stack(k_end), jnp.exp(pre[nbc])

    def advance(bi, prepared):
        attn, q_start, k_end, decay = prepared
        rows = slice(bi * blk, (bi + 1) * blk)
        v = v_ref[0, rows, :]
        st = st_ref[...]
        o = (jnp.dot(attn, v, preferred_element_type=F32)
             + lax.dot_general(q_start, st.astype(BF16), _NT, preferred_element_type=F32))
        st_ref[...] = st * decay + lax.dot_general(v, k_end, _TN, preferred_element_type=F32)
        gr = gr_ref[0, rows, :].astype(F32)
        y = _rms(o, g_ref[...]) * (gr * jax.nn.sigmoid(gr))
        o_ref[0, rows, :] = y.astype(o_ref.dtype)

    return prepare, advance


def _mixers_body(mq_ref, mk_ref, mv_ref, cos_ref, sin_ref, gq_ref, gk_ref, gv_ref, gr_ref, ga_ref,
                 up_ref, bias_ref, g_ref, om_ref, og_ref, st_ref, la_ref, *, nb, moba_heads, dk):
    moba_scores, moba_finish = _moba_setup(mq_ref, mk_ref, mv_ref, cos_ref, sin_ref, om_ref,
                                           nb=nb, heads=moba_heads)
    gla_prepare, gla_advance = _gla_setup(gq_ref, gk_ref, gv_ref, gr_ref, ga_ref, up_ref, bias_ref,
                                          g_ref, og_ref, st_ref, la_ref, dk=dk)
    s_next = moba_scores(0)
    for i in range(nb):
        s_now, s_next = s_next, (moba_scores(i + 1) if i + 1 < nb else None)
        gla_advance(i, gla_prepare(i))
        moba_finish(i, s_now)


def _mixers(u, cos, sin, ga, up, bias, g, batch, seq, moba_heads, gla_heads, dk, dv,
            q_off, k_off, v_off, r_off):
    hd = MOBA_HEAD_DIM
    assert MOBA_BLOCK == GLA_CHUNK * GLA_BLOCK_CHUNKS and moba_heads % gla_heads == 0
    per = moba_heads // gla_heads
    nb = seq // MOBA_BLOCK
    mb = lambda off: pl.BlockSpec((1, seq, per * hd), lambda b, h: (b, 0, h + off))
    qk = lambda off: pl.BlockSpec((1, seq, dk), lambda b, h: (b, 0, h + off))
    vv = lambda off: pl.BlockSpec((1, seq, dv), lambda b, h: (b, 0, h + off))
    tab = pl.BlockSpec((seq, hd), lambda b, h: (0, 0))
    return pl.pallas_call(
        functools.partial(_mixers_body, nb=nb, moba_heads=per, dk=dk),
        grid=(batch, gla_heads),
        in_specs=[mb(0), mb(gla_heads), mb(2 * gla_heads), tab, tab,
                  qk(q_off), qk(k_off), vv(v_off), vv(r_off),
                  pl.BlockSpec((1, seq, LANES), lambda b, h: (b, 0, 0)),
                  pl.BlockSpec((LANES, dk), lambda b, h: (0, h)),
                  pl.BlockSpec((1, dk), lambda b, h: (0, h)),
                  pl.BlockSpec((1, dv), lambda b, h: (0, 0))],
        out_specs=[pl.BlockSpec((1, seq, per * hd), lambda b, h: (b, 0, h)),
                   pl.BlockSpec((1, seq, dv), lambda b, h: (b, 0, h))],
        out_shape=[jax.ShapeDtypeStruct((batch, seq, moba_heads * hd), BF16),
                   jax.ShapeDtypeStruct((batch, seq, gla_heads * dv), BF16)],
        scratch_shapes=[pltpu.VMEM((dv, dk), F32),
                        pltpu.VMEM((seq, dk), F32)],
        compiler_params=_params("parallel", "parallel"),
        name="token_mixers",
    )(u, u, u, cos, sin, u, u, u, u, ga, up, bias, g)


def _tile(n, pref):
    return pref if n % pref == 0 else n


def _layer(x, pre_mix_g, w_in, gate_up, gate_bias, gla_norm_g, w_moba, w_gla, w_out,
           post_mix_g, pre_ffn_g, w_fg, w_fu, w_fd, post_ffn_g):
    batch, seq, d = x.shape
    t = batch * seq
    half = d // 2
    moba_heads = half // MOBA_HEAD_DIM
    gla_heads = d // 512
    dk = half // gla_heads
    dv = d // gla_heads
    n_main = 9 * half
    n_gate0 = n_main + GLA_GATE_RANK
    d_ff = w_fg.shape[1]

    tm = _tile(t, 1024)
    tn = _tile(d, 1024)
    tr = _tile(t, 512)
    x2 = x.reshape(t, d)
    vec = lambda g: g.reshape(1, -1)

    w_in_t = w_in.T
    w_ga = jnp.pad(jnp.tile(w_in_t[n_main:n_gate0], (3, 1)),
                   ((0, LANES - 3 * GLA_GATE_RANK), (0, 0))).T
    up_pad = jnp.pad(jnp.tile(gate_up, (3, 1)), ((0, LANES - 3 * GLA_GATE_RANK), (0, 0)))

    hd2 = MOBA_HEAD_DIM // 2
    inv_freq = ROPE_THETA ** (-jnp.arange(hd2, dtype=F32) / hd2)
    ang = jnp.arange(seq, dtype=F32)[:, None] * inv_freq[None, :]
    cos = jnp.concatenate([jnp.cos(ang), jnp.cos(ang)], axis=1)
    sin = jnp.concatenate([-jnp.sin(ang), jnp.sin(ang)], axis=1)

    h, ga = _rmsnorm_proj(x2, vec(pre_mix_g), w_ga, tr)
    u = _matmul(h, w_in_t, n_main, tm, 1024 if n_main % 1024 == 0 else 512, BF16, "in_proj_main",
                transposed=True)
    gates = _matmul(h, w_in_t, 2 * d, tm, tn, BF16, "in_proj_gates", col0=n_gate0, transposed=True)

    u3 = u.reshape(batch, seq, n_main)
    o_moba, o_gla = _mixers(u3, cos, sin, ga.reshape(batch, seq, LANES), up_pad, vec(gate_bias),
                            vec(gla_norm_g), batch, seq, moba_heads, gla_heads, dk, dv,
                            q_off=3 * half // dk, k_off=4 * half // dk, v_off=5 * half // dv,
                            r_off=7 * half // dv)

    merged = _merge(o_moba.reshape(t, half), o_gla.reshape(t, d), w_moba, w_gla, gates, tm, 512)
    y = _matmul(merged, w_out, d, tm, tn, BF16, "out_proj")
    x1, h2 = _mid_norm(y, x2, vec(post_mix_g), vec(pre_ffn_g), tr)

    a, w_fd_bf = _ffn_up(h2, w_fg, w_fu, w_fd, tm, 512)
    y2 = _matmul_bf16(a, w_fd_bf, _tile(t, 512), 512, BF16, "ffn_down")
    out = _final_norm(y2, x1, vec(post_ffn_g), tr)
    return out.reshape(batch, seq, d)


def kernel(x, pre_mix_norm_g, w_in, gla_gate_up, gla_gate_bias, gla_out_norm_g, w_branch_moba,
           w_branch_gla, w_out, post_mix_norm_g, pre_ffn_norm_g, w_ffn_gate, w_ffn_up, w_ffn_down,
           post_ffn_norm_g):
    for layer in range(w_in.shape[0]):
        x = _layer(x, pre_mix_norm_g[layer], w_in[layer], gla_gate_up[layer], gla_gate_bias[layer],
                   gla_out_norm_g[layer], w_branch_moba[layer], w_branch_gla[layer], w_out[layer],
                   post_mix_norm_g[layer], pre_ffn_norm_g[layer], w_ffn_gate[layer],
                   w_ffn_up[layer], w_ffn_down[layer], post_ffn_norm_g[layer])
    return x
```

```python
import functools

import jax
import jax.numpy as jnp
from jax import lax
from jax.experimental import pallas as pl
from jax.experimental.pallas import tpu as pltpu

F32 = jnp.float32
BF16 = jnp.bfloat16

EPS = 1e-6
MOBA_HEAD_DIM = 128
MOBA_BLOCK = 256
MOBA_TOPK = 3
ROPE_THETA = 10000.0
LOG2_E = 1.4426950408889634
MASKED = -1e30
GLA_GATE_RANK = 16
GLA_GATE_TAU = 16.0
GLA_CHUNK = 64
GLA_BLOCK_CHUNKS = 4
GLA_VALUE_DIM = 512
LANES = 128
VMEM_LIMIT = 60 * 1024 * 1024
ROW_TILE = 1024
COL_TILE = 1024
HALF_TILE = 512

_NT = (((1,), (1,)), ((), ()))
_TN = (((0,), (0,)), ((), ()))


def _params(*sem):
    return pltpu.CompilerParams(dimension_semantics=sem, vmem_limit_bytes=VMEM_LIMIT)


def _rms(x, g):
    ms = jnp.mean(x * x, axis=-1, keepdims=True)
    return x * lax.rsqrt(ms + EPS) * g


def _rmsnorm_proj_body(x_ref, g_ref, w_ref, o_ref, p_ref):
    h = _rms(x_ref[...], g_ref[...]).astype(o_ref.dtype)
    o_ref[...] = h
    p_ref[...] = jnp.dot(h, w_ref[...].astype(BF16), preferred_element_type=F32)


def _rmsnorm_proj(x, g, w, tm):
    m, d = x.shape
    n = w.shape[1]
    return pl.pallas_call(
        _rmsnorm_proj_body,
        grid=(m // tm,),
        in_specs=[pl.BlockSpec((tm, d), lambda i: (i, 0)),
                  pl.BlockSpec((1, d), lambda i: (0, 0)),
                  pl.BlockSpec((d, n), lambda i: (0, 0))],
        out_specs=[pl.BlockSpec((tm, d), lambda i: (i, 0)), pl.BlockSpec((tm, n), lambda i: (i, 0))],
        out_shape=[jax.ShapeDtypeStruct((m, d), BF16), jax.ShapeDtypeStruct((m, n), F32)],
        compiler_params=_params("parallel"),
        name="rmsnorm_forget_proj",
    )(x, g, w)


def _mid_norm_body(y_ref, x_ref, g1_ref, g2_ref, x1_ref, h_ref):
    x1 = x_ref[...] + _rms(y_ref[...].astype(F32), g1_ref[...])
    x1_ref[...] = x1
    h_ref[...] = _rms(x1, g2_ref[...]).astype(h_ref.dtype)


def _mid_norm(y, x, g1, g2, tm):
    m, d = x.shape
    row = pl.BlockSpec((tm, d), lambda i: (i, 0))
    vec = pl.BlockSpec((1, d), lambda i: (0, 0))
    return pl.pallas_call(
        _mid_norm_body,
        grid=(m // tm,),
        in_specs=[row, row, vec, vec],
        out_specs=[row, row],
        out_shape=[jax.ShapeDtypeStruct((m, d), F32), jax.ShapeDtypeStruct((m, d), BF16)],
        compiler_params=_params("parallel"),
        name="mid_norm",
    )(y, x, g1, g2)


def _final_norm_body(y_ref, x_ref, g_ref, o_ref):
    o_ref[...] = x_ref[...] + _rms(y_ref[...].astype(F32), g_ref[...])


def _final_norm(y, x, g, tm):
    m, d = x.shape
    row = pl.BlockSpec((tm, d), lambda i: (i, 0))
    return pl.pallas_call(
        _final_norm_body,
        grid=(m // tm,),
        in_specs=[row, row, pl.BlockSpec((1, d), lambda i: (0, 0))],
        out_specs=row,
        out_shape=jax.ShapeDtypeStruct((m, d), F32),
        compiler_params=_params("parallel"),
        name="final_norm",
    )(y, x, g)


def _tile_copy(w_hbm, stage, sem, jj, *, tn, n, col0, transposed, ragged):
    width = n % tn if ragged else tn
    first = col0 + jj * tn
    if transposed:
        return pltpu.make_async_copy(w_hbm.at[pl.ds(first, width), :],
                                     stage.at[pl.ds(0, width), :], sem)
    return pltpu.make_async_copy(w_hbm.at[:, pl.ds(first, width)],
                                 stage.at[:, pl.ds(0, width)], sem)


def _stage_weight(w_hbm, stage, wbf, sem, *, tn, n, col0=0, transposed=False):
    j = pl.program_id(0)
    nj = pl.num_programs(0)
    copy = functools.partial(_tile_copy, w_hbm, stage, sem, tn=tn, n=n, col0=col0,
                             transposed=transposed)

    def for_tile(jj, act):
        if n % tn:
            pl.when(jj < nj - 1)(lambda: act(copy(jj, ragged=False)))
            pl.when(jj == nj - 1)(lambda: act(copy(jj, ragged=True)))
        else:
            act(copy(jj, ragged=False))

    @pl.when(pl.program_id(1) == 0)
    def _():
        pl.when(j == 0)(lambda: for_tile(j, lambda c: c.start()))
        for_tile(j, lambda c: c.wait())
        wbf[...] = stage[...].astype(BF16)
        pl.when(j + 1 < nj)(lambda: for_tile(j + 1, lambda c: c.start()))


def _weight_scratch(k, tn, transposed=False):
    shape = (tn, k) if transposed else (k, tn)
    return [pltpu.VMEM(shape, F32), pltpu.VMEM(shape, BF16)]


_HBM = pl.BlockSpec(memory_space=pl.ANY)


def _mm_body(x_ref, w_hbm, o_ref, stage, wbf, sem, **tiles):
    _stage_weight(w_hbm, stage, wbf, sem.at[0], **tiles)
    dims = _NT if tiles["transposed"] else (((1,), (0,)), ((), ()))
    o_ref[...] = lax.dot_general(x_ref[...], wbf[...], dims,
                                 preferred_element_type=F32).astype(o_ref.dtype)


def _matmul(x, w, n, tm, tn, out_dtype, name, col0=0, transposed=False):
    m, k = x.shape
    return pl.pallas_call(
        functools.partial(_mm_body, tn=tn, n=n, col0=col0, transposed=transposed),
        grid=(n // tn, m // tm),
        in_specs=[pl.BlockSpec((tm, k), lambda j, i: (i, 0)), _HBM],
        out_specs=pl.BlockSpec((tm, tn), lambda j, i: (i, j)),
        out_shape=jax.ShapeDtypeStruct((m, n), out_dtype),
        scratch_shapes=_weight_scratch(k, tn, transposed) + [pltpu.SemaphoreType.DMA((1,))],
        compiler_params=_params("arbitrary", "arbitrary"),
        name=name,
    )(x, w)


def _merge_body(om_ref, og_ref, wm_hbm, wg_hbm, gm_ref, gg_ref, o_ref,
                wm_stage, wm_bf, wg_stage, wg_bf, sem, **tiles):
    _stage_weight(wm_hbm, wm_stage, wm_bf, sem.at[0], **tiles)
    _stage_weight(wg_hbm, wg_stage, wg_bf, sem.at[1], **tiles)
    ym = jnp.dot(om_ref[...], wm_bf[...], preferred_element_type=F32)
    yg = jnp.dot(og_ref[...], wg_bf[...], preferred_element_type=F32)
    merged = (jax.nn.sigmoid(gm_ref[...].astype(F32)) * ym
              + jax.nn.sigmoid(gg_ref[...].astype(F32)) * yg)
    o_ref[...] = merged.astype(o_ref.dtype)


def _merge(o_moba, o_gla, w_moba, w_gla, gates, tm, tn):
    m, km = o_moba.shape
    kg = o_gla.shape[1]
    d = w_moba.shape[1]
    nj = d // tn
    return pl.pallas_call(
        functools.partial(_merge_body, tn=tn, n=d),
        grid=(nj, m // tm),
        in_specs=[pl.BlockSpec((tm, km), lambda j, i: (i, 0)),
                  pl.BlockSpec((tm, kg), lambda j, i: (i, 0)),
                  _HBM, _HBM,
                  pl.BlockSpec((tm, tn), lambda j, i: (i, j)),
                  pl.BlockSpec((tm, tn), lambda j, i: (i, j + nj))],
        out_specs=pl.BlockSpec((tm, tn), lambda j, i: (i, j)),
        out_shape=jax.ShapeDtypeStruct((m, d), BF16),
        scratch_shapes=(_weight_scratch(km, tn) + _weight_scratch(kg, tn)
                        + [pltpu.SemaphoreType.DMA((2,))]),
        compiler_params=_params("arbitrary", "arbitrary"),
        name="branch_merge",
    )(o_moba, o_gla, w_moba, w_gla, gates, gates)


def _ffn_up_body(h_ref, wg_hbm, wu_hbm, wd_ref, o_ref, wd_bf_ref,
                 wg_stage, wg_bf, wu_stage, wu_bf, sem, **tiles):
    _stage_weight(wg_hbm, wg_stage, wg_bf, sem.at[0], **tiles)
    _stage_weight(wu_hbm, wu_stage, wu_bf, sem.at[1], **tiles)

    def compute(width):
        h = h_ref[...]
        a = jnp.dot(h, wg_bf[:, :width], preferred_element_type=F32)
        b = jnp.dot(h, wu_bf[:, :width], preferred_element_type=F32)
        o_ref[:, :width] = (a * jax.nn.sigmoid(a) * b).astype(o_ref.dtype)

    tn, short = tiles["tn"], tiles["n"] % tiles["tn"]
    if short:
        last = pl.num_programs(0) - 1
        pl.when(pl.program_id(0) < last)(lambda: compute(tn))
        pl.when(pl.program_id(0) == last)(lambda: compute(short))
    else:
        compute(tn)
    wd_bf_ref[...] = wd_ref[...].astype(BF16)


def _ffn_up(h, w_gate, w_up, w_down, tm, tn):
    m, k = h.shape
    n = w_gate.shape[1]
    ni = m // tm
    steps = pl.cdiv(n, tn) * ni
    slab = -(-pl.cdiv(w_down.shape[0], steps) // 16) * 16
    last = pl.cdiv(w_down.shape[0], slab) - 1
    wd_spec = pl.BlockSpec((slab, w_down.shape[1]), lambda j, i: (jnp.minimum(j * ni + i, last), 0))
    return pl.pallas_call(
        functools.partial(_ffn_up_body, tn=tn, n=n),
        grid=(pl.cdiv(n, tn), ni),
        in_specs=[pl.BlockSpec((tm, k), lambda j, i: (i, 0)), _HBM, _HBM, wd_spec],
        out_specs=[pl.BlockSpec((tm, tn), lambda j, i: (i, j)), wd_spec],
        out_shape=[jax.ShapeDtypeStruct((m, n), BF16), jax.ShapeDtypeStruct(w_down.shape, BF16)],
        scratch_shapes=(_weight_scratch(k, tn) + _weight_scratch(k, tn)
                        + [pltpu.SemaphoreType.DMA((2,))]),
        compiler_params=_params("arbitrary", "arbitrary"),
        name="ffn_up",
    )(h, w_gate, w_up, w_down)


def _mm_bf16_body(x_ref, w_ref, o_ref):
    o_ref[...] = jnp.dot(x_ref[...], w_ref[...], preferred_element_type=F32).astype(o_ref.dtype)


def _matmul_bf16(x, w, tm, tn, out_dtype, name):
    m, k = x.shape
    n = w.shape[1]
    return pl.pallas_call(
        _mm_bf16_body,
        grid=(n // tn, m // tm),
        in_specs=[pl.BlockSpec((tm, k), lambda j, i: (i, 0)),
                  pl.BlockSpec((k, tn), lambda j, i: (0, j))],
        out_specs=pl.BlockSpec((tm, tn), lambda j, i: (i, j)),
        out_shape=jax.ShapeDtypeStruct((m, n), out_dtype),
        compiler_params=_params("parallel", "parallel"),
        name=name,
    )(x, w)


def _moba_setup(q_ref, k_ref, v_ref, cos_ref, sin_ref, o_ref, *, nb, heads):
    blk = MOBA_BLOCK
    hd = MOBA_HEAD_DIM
    seq = q_ref.shape[1]
    scale = hd ** -0.5
    cos = cos_ref[...]
    sin = sin_ref[...]
    row = lax.broadcasted_iota(jnp.int32, (blk, blk), 0)
    col = lax.broadcasted_iota(jnp.int32, (blk, blk), 1)
    causal = col <= row

    r_l = lax.broadcasted_iota(jnp.int32, (LANES, LANES), 0)
    c_l = lax.broadcasted_iota(jnp.int32, (LANES, LANES), 1)
    sum_wins = jnp.where((c_l == r_l % nb) & (r_l < nb * nb), 1.0, 0.0).astype(BF16)
    key_lane = lax.broadcasted_iota(jnp.int32, (seq, LANES), 1)
    key_block = lax.broadcasted_iota(jnp.int32, (seq, LANES), 0) // blk
    block_ind = jnp.where(key_lane == key_block, 1.0, 0.0).astype(BF16)
    ones_col = jnp.where(key_lane == 0, 1.0, 0.0).astype(BF16)

    lane = lax.broadcasted_iota(jnp.int32, (blk, LANES), 1)
    lane_j = lane // nb
    lane_n = lane % nb
    lane_m = (lane_n - lane_j + nb) % nb
    tie = jnp.where(lane_m < lane_n, 1.0, 0.0)

    def rope(t):
        t = t.astype(F32)
        return t * cos + pltpu.roll(t, hd // 2, axis=1) * sin

    def prepare(h):
        cols = slice(h * hd, (h + 1) * hd)
        qr = rope(q_ref[0, :, cols])
        kr = rope(k_ref[0, :, cols])
        kmean = jnp.mean(kr.reshape(nb, blk, hd), axis=1)
        parts = [kmean] + [jnp.roll(kmean, j, axis=0) - kmean for j in range(1, nb)]
        parts.append(jnp.zeros((LANES - nb * nb, hd), F32))
        diff = jnp.concatenate(parts, axis=0)
        q_hi = qr.astype(BF16)
        q_lo = (qr - q_hi.astype(F32)).astype(BF16)
        d_hi = diff.astype(BF16)
        d_lo = (diff - d_hi.astype(F32)).astype(BF16)
        gate = lax.dot_general(jnp.concatenate([q_hi, q_lo, q_hi], axis=1),
                               jnp.concatenate([d_hi, d_hi, d_lo], axis=1), _NT,
                               preferred_element_type=F32)
        return dict(
            cols=cols, gate=gate,
            q=(qr * (scale * LOG2_E)).astype(BF16),
            k=jnp.concatenate([kr.astype(BF16), block_ind], axis=1),
            v=jnp.concatenate([v_ref[0, :, cols], ones_col], axis=1))

    def block_bias(gate, i):
        finite = jnp.abs(gate) < jnp.inf
        if i <= MOBA_TOPK:
            keep = finite
        else:
            compared = (lane_j >= 1) & (lane_j < nb) & (lane_n < i) & (lane_m < i)
            wins = jnp.where(compared, jnp.where(gate > 0.0, 1.0, jnp.where(gate == 0.0, tie, 0.0)), 0.0)
            wins = jnp.where(lane < nb, jnp.where(finite, 0.0, float(MOBA_TOPK)), wins)
            rank = jnp.dot(wins.astype(BF16), sum_wins, preferred_element_type=F32)
            keep = rank < MOBA_TOPK
        return jnp.where(lane < i, jnp.where(keep, 0.0, MASKED), jnp.where(lane == i, 0.0, MASKED))

    def probabilities(s, i):
        own = jnp.where(causal, s[:, i * blk:], -jnp.inf)
        s = jnp.concatenate([s[:, :i * blk], own], axis=1) if i else own
        m = jnp.max(s, axis=1, keepdims=True)
        return jnp.exp2(s - m).astype(BF16)

    hs = [prepare(h) for h in range(heads)]

    def scores(i):
        rows = slice(i * blk, (i + 1) * blk)
        kv = slice(0, (i + 1) * blk)
        q_aug = [jnp.concatenate([h["q"][rows], block_bias(h["gate"][rows], i).astype(BF16)], axis=1)
                 for h in hs]
        return [lax.dot_general(q_aug[n], h["k"][kv], _NT, preferred_element_type=F32)
                for n, h in enumerate(hs)]

    def finish(i, s):
        rows = slice(i * blk, (i + 1) * blk)
        kv = slice(0, (i + 1) * blk)
        p = [probabilities(s[n], i) for n in range(heads)]
        for n, h in enumerate(hs):
            o = jnp.dot(p[n], h["v"][kv], preferred_element_type=F32)
            o_ref[0, rows, h["cols"]] = (o[:, :hd] / o[:, hd:hd + 1]).astype(o_ref.dtype)

    return scores, finish


def _split2(x):
    hi = x.astype(BF16)
    return hi, (x - hi.astype(F32)).astype(BF16)


def _gla_setup(q_ref, k_ref, v_ref, gr_ref, ga_ref, up_ref, bias_ref, g_ref, o_ref,
               st_ref, la_ref, *, dk):
    c = GLA_CHUNK
    nbc = GLA_BLOCK_CHUNKS
    mid = nbc // 2
    blk = c * nbc
    scale = dk ** -0.5
    row = lax.broadcasted_iota(jnp.int32, (blk, blk), 0)
    col = lax.broadcasted_iota(jnp.int32, (blk, blk), 1)
    shift = c.bit_length() - 1
    row_c = jnp.right_shift(row, shift)
    col_c = jnp.right_shift(col, shift)
    below = row_c > col_c
    diag = (row_c == col_c) & (col <= row)
    tri = jnp.where(diag, 1.0, 0.0).astype(BF16)

    rank = GLA_GATE_RANK
    g = ga_ref[0]
    g_hi = g.astype(BF16).astype(F32)
    lane = lax.broadcasted_iota(jnp.int32, g.shape, 1)
    lhs = jnp.where((lane >= rank) & (lane < 2 * rank), g - g_hi, g_hi).astype(BF16)
    u = up_ref[...]
    u_hi = u.astype(BF16).astype(F32)
    rhs = jnp.where(lax.broadcasted_iota(jnp.int32, u.shape, 0) >= 2 * rank, u - u_hi,
                    u_hi).astype(BF16)
    z = jnp.dot(lhs, rhs, preferred_element_type=F32) + bias_ref[...]
    la_ref[...] = (jnp.minimum(z, 0.0) - jnp.log(1.0 + jnp.exp(-jnp.abs(z)))) / GLA_GATE_TAU
    st_ref[...] = jnp.zeros_like(st_ref)

    def stack(parts):
        return jnp.concatenate(parts, axis=0).astype(BF16)

    def prepare(bi):
        rows = slice(bi * blk, (bi + 1) * blk)
        hi, lo = _split2(la_ref[rows, :])
        b = (jnp.dot(tri, hi, preferred_element_type=F32)
             + jnp.dot(tri, lo, preferred_element_type=F32))
        q = q_ref[0, rows, :].astype(F32) * scale
        k = k_ref[0, rows, :].astype(F32)
        q_dec = q * jnp.exp(b)
        k_inv = k * jnp.exp(-b)
        tot = [b[(j + 1) * c - 1:(j + 1) * c, :] for j in range(nbc)]
        pre = [jnp.zeros_like(tot[0])]
        for j in range(nbc):
            pre.append(pre[-1] + tot[j])
        q_mid, k_mid, q_start, k_end = [], [], [], []
        for j in range(nbc):
            sl = slice(j * c, (j + 1) * c)
            q_mid.append(q_dec[sl] * jnp.exp(pre[j] - pre[mid]) if j else q_dec[sl])
            k_mid.append(k_inv[sl] * jnp.exp(pre[mid] - pre[j]) if j < nbc - 1 else k_inv[sl])
            q_start.append(q_dec[sl] * jnp.exp(pre[j]))
            k_end.append(k[sl] * jnp.exp((pre[nbc] - pre[j]) - b[sl]))
        a_diag = lax.dot_general(q_dec.astype(BF16), k_inv.astype(BF16), _NT,
                                 preferred_element_type=F32)
        a_off = lax.dot_general(stack(q_mid), stack(k_mid), _NT, preferred_element_type=F32)
        attn = jnp.where(below, a_off, jnp.where(diag, a_diag, 0.0)).astype(BF16)
        return attn, stack(q_start), stack(k_end), jnp.exp(pre[nbc])

    def advance(bi, prepared):
        attn, q_start, k_end, decay = prepared
        rows = slice(bi * blk, (bi + 1) * blk)
        v = v_ref[0, rows, :]
        st = st_ref[...]
        o = (jnp.dot(attn, v, preferred_element_type=F32)
             + lax.dot_general(q_start, st.astype(BF16), _NT, preferred_element_type=F32))
        st_ref[...] = st * decay + lax.dot_general(v, k_end, _TN, preferred_element_type=F32)
        gr = gr_ref[0, rows, :].astype(F32)
        y = _rms(o, g_ref[...]) * (gr * jax.nn.sigmoid(gr))
        o_ref[0, rows, :] = y.astype(o_ref.dtype)

    return prepare, advance


def _mixers_body(mq_ref, mk_ref, mv_ref, cos_ref, sin_ref, gq_ref, gk_ref, gv_ref, gr_ref, ga_ref,
                 up_ref, bias_ref, g_ref, om_ref, og_ref, st_ref, la_ref, *, nb, moba_heads, dk):
    moba_scores, moba_finish = _moba_setup(mq_ref, mk_ref, mv_ref, cos_ref, sin_ref, om_ref,
                                           nb=nb, heads=moba_heads)
    gla_prepare, gla_advance = _gla_setup(gq_ref, gk_ref, gv_ref, gr_ref, ga_ref, up_ref, bias_ref,
                                          g_ref, og_ref, st_ref, la_ref, dk=dk)
    s_next = moba_scores(0)
    for i in range(nb):
        s_now, s_next = s_next, (moba_scores(i + 1) if i + 1 < nb else None)
        gla_advance(i, gla_prepare(i))
        moba_finish(i, s_now)


def _mixers(u, cos, sin, ga, up, bias, g, batch, seq, moba_heads, gla_heads, dk, dv,
            q_off, k_off, v_off, r_off):
    hd = MOBA_HEAD_DIM
    assert MOBA_BLOCK == GLA_CHUNK * GLA_BLOCK_CHUNKS and moba_heads % gla_heads == 0
    per = moba_heads // gla_heads
    nb = seq // MOBA_BLOCK
    mb = lambda off: pl.BlockSpec((1, seq, per * hd), lambda b, h: (b, 0, h + off))
    qk = lambda off: pl.BlockSpec((1, seq, dk), lambda b, h: (b, 0, h + off))
    vv = lambda off: pl.BlockSpec((1, seq, dv), lambda b, h: (b, 0, h + off))
    tab = pl.BlockSpec((seq, hd), lambda b, h: (0, 0))
    return pl.pallas_call(
        functools.partial(_mixers_body, nb=nb, moba_heads=per, dk=dk),
        grid=(batch, gla_heads),
        in_specs=[mb(0), mb(gla_heads), mb(2 * gla_heads), tab, tab,
                  qk(q_off), qk(k_off), vv(v_off), vv(r_off),
                  pl.BlockSpec((1, seq, LANES), lambda b, h: (b, 0, 0)),
                  pl.BlockSpec((LANES, dk), lambda b, h: (0, h)),
                  pl.BlockSpec((1, dk), lambda b, h: (0, h)),
                  pl.BlockSpec((1, dv), lambda b, h: (0, 0))],
        out_specs=[pl.BlockSpec((1, seq, per * hd), lambda b, h: (b, 0, h)),
                   pl.BlockSpec((1, seq, dv), lambda b, h: (b, 0, h))],
        out_shape=[jax.ShapeDtypeStruct((batch, seq, moba_heads * hd), BF16),
                   jax.ShapeDtypeStruct((batch, seq, gla_heads * dv), BF16)],
        scratch_shapes=[pltpu.VMEM((dv, dk), F32),
                        pltpu.VMEM((seq, dk), F32)],
        compiler_params=_params("parallel", "parallel"),
        name="token_mixers",
    )(u, u, u, cos, sin, u, u, u, u, ga, up, bias, g)


def _tile(n, pref):
    return pref if n % pref == 0 else n


def _layer(x, pre_mix_g, w_in, gate_up, gate_bias, gla_norm_g, w_moba, w_gla, w_out,
           post_mix_g, pre_ffn_g, w_fg, w_fu, w_fd, post_ffn_g):
    batch, seq, d = x.shape
    t = batch * seq
    half = d // 2
    moba_heads = half // MOBA_HEAD_DIM
    gla_heads = d // GLA_VALUE_DIM
    dk = half // gla_heads
    dv = d // gla_heads
    n_main = 9 * half
    n_gate0 = n_main + GLA_GATE_RANK

    tm = _tile(t, ROW_TILE)
    tn = _tile(d, COL_TILE)
    th = _tile(t, HALF_TILE)
    x2 = x.reshape(t, d)
    vec = lambda g: g.reshape(1, -1)

    w_in_t = w_in.T
    w_ga = jnp.pad(jnp.tile(w_in_t[n_main:n_gate0], (3, 1)),
                   ((0, LANES - 3 * GLA_GATE_RANK), (0, 0))).T
    up_pad = jnp.pad(jnp.tile(gate_up, (3, 1)), ((0, LANES - 3 * GLA_GATE_RANK), (0, 0)))

    hd2 = MOBA_HEAD_DIM // 2
    inv_freq = ROPE_THETA ** (-jnp.arange(hd2, dtype=F32) / hd2)
    ang = jnp.arange(seq, dtype=F32)[:, None] * inv_freq[None, :]
    cos = jnp.concatenate([jnp.cos(ang), jnp.cos(ang)], axis=1)
    sin = jnp.concatenate([-jnp.sin(ang), jnp.sin(ang)], axis=1)

    h, ga = _rmsnorm_proj(x2, vec(pre_mix_g), w_ga, th)
    u = _matmul(h, w_in_t, n_main, tm, COL_TILE if n_main % COL_TILE == 0 else HALF_TILE, BF16,
                "in_proj_main", transposed=True)
    gates = _matmul(h, w_in_t, 2 * d, tm, tn, BF16, "in_proj_gates", col0=n_gate0, transposed=True)

    u3 = u.reshape(batch, seq, n_main)
    o_moba, o_gla = _mixers(u3, cos, sin, ga.reshape(batch, seq, LANES), up_pad, vec(gate_bias),
                            vec(gla_norm_g), batch, seq, moba_heads, gla_heads, dk, dv,
                            q_off=3 * half // dk, k_off=4 * half // dk, v_off=5 * half // dv,
                            r_off=7 * half // dv)

    merged = _merge(o_moba.reshape(t, half), o_gla.reshape(t, d), w_moba, w_gla, gates, tm,
                    _tile(d, HALF_TILE))
    y = _matmul(merged, w_out, d, tm, tn, BF16, "out_proj")
    x1, h2 = _mid_norm(y, x2, vec(post_mix_g), vec(pre_ffn_g), th)

    a, w_fd_bf = _ffn_up(h2, w_fg, w_fu, w_fd, tm, HALF_TILE)
    y2 = _matmul_bf16(a, w_fd_bf, th, _tile(d, HALF_TILE), BF16, "ffn_down")
    out = _final_norm(y2, x1, vec(post_ffn_g), th)
    return out.reshape(batch, seq, d)


def kernel(x, pre_mix_norm_g, w_in, gla_gate_up, gla_gate_bias, gla_out_norm_g, w_branch_moba,
           w_branch_gla, w_out, post_mix_norm_g, pre_ffn_norm_g, w_ffn_gate, w_ffn_up, w_ffn_down,
           post_ffn_norm_g):
    for layer in range(w_in.shape[0]):
        x = _layer(x, pre_mix_norm_g[layer], w_in[layer], gla_gate_up[layer], gla_gate_bias[layer],
                   gla_out_norm_g[layer], w_branch_moba[layer], w_branch_gla[layer], w_out[layer],
                   post_mix_norm_g[layer], pre_ffn_norm_g[layer], w_ffn_gate[layer],
                   w_ffn_up[layer], w_ffn_down[layer], post_ffn_norm_g[layer])
    return x
```
